```python
import jax, jax.numpy as jnp
from jax import lax
import numpy as np

D_MODEL = 4096
BATCH = 4
SEQ = 4096
DEPTH = 2

CHUNK = 64
BRANCH_WIDTH = D_MODEL // 2
RG_WIDTH = BRANCH_WIDTH
RG_BLOCKS = 16
RG_BLOCK_DIM = RG_WIDTH // RG_BLOCKS
RG_CONV = 4
RG_C = 8.0
SB_HEADS = 16
SB_HEAD_DIM = BRANCH_WIDTH // SB_HEADS
SB_WIDTH = SB_HEADS * SB_HEAD_DIM
Q_BLOCK = 128
N_BRANCH = 2
W_IN = 2 * RG_WIDTH + 3 * SB_WIDTH
D_FF = 3 * D_MODEL
FFN_CONV = 3
PLE_DIM = 256
EPS = 1e-6

kernel_name = "hybrid_rglru_stickbreaking_convffn_ple"


def rms_norm(x, g):
    xf = x.astype(jnp.float32)
    y = xf * lax.rsqrt(jnp.mean(xf * xf, axis=-1, keepdims=True) + EPS)
    return (y * g.astype(jnp.float32)).astype(x.dtype)


def causal_dwconv(x, w, b):
    K, C = w.shape
    y = lax.conv_general_dilated(
        x, w[:, None, :].astype(x.dtype), window_strides=(1,), padding=[(K - 1, 0)],
        dimension_numbers=("NWC", "WIO", "NWC"), feature_group_count=C)
    return y + b.astype(x.dtype)


def _linear_combine(left, right):
    a1, b1 = left
    a2, b2 = right
    return a1 * a2, a2 * b1 + b2


def rg_lru(x, w_a, b_a, w_x, b_x, lam):
    B, S, C = x.shape
    xb = x.reshape(B, S, RG_BLOCKS, RG_BLOCK_DIM)
    r = jax.nn.sigmoid(jnp.einsum("bshi,hij->bshj", xb, w_a).reshape(B, S, C) + b_a)
    i_gate = jax.nn.sigmoid(jnp.einsum("bshi,hij->bshj", xb, w_x).reshape(B, S, C) + b_x)
    log_a = -RG_C * r.astype(jnp.float32) * jax.nn.softplus(-lam.astype(jnp.float32))
    a = jnp.exp(log_a)
    mult = jnp.sqrt(-jnp.expm1(2.0 * log_a))
    u = x.astype(jnp.float32) * i_gate.astype(jnp.float32) * mult
    _, h = lax.associative_scan(_linear_combine, (a, u), axis=1)
    return h.astype(x.dtype)


def stick_breaking_attention(q, k, v):
    B, S, H, Dh = q.shape
    scale = Dh ** -0.5
    outs = []
    for blk in range(S // Q_BLOCK):
        q0 = blk * Q_BLOCK
        kv_len = q0 + Q_BLOCK
        qb = q[:, q0:kv_len].astype(jnp.float32)
        kb = k[:, :kv_len].astype(jnp.float32)
        vb = v[:, :kv_len].astype(jnp.float32)
        z = jnp.einsum("bqhd,bkhd->bhqk", qb, kb) * scale
        t_idx = q0 + jnp.arange(Q_BLOCK)[:, None]
        s_idx = jnp.arange(kv_len)[None, :]
        mask = s_idx < t_idx
        log_beta = jax.nn.log_sigmoid(z)
        log_rest = jnp.where(mask, jax.nn.log_sigmoid(-z), 0.0)
        suffix = lax.cumsum(log_rest, axis=3, reverse=True) - log_rest
        A = jnp.where(mask, jnp.exp(log_beta + suffix), 0.0)
        outs.append(jnp.einsum("bhqk,bkhd->bqhd", A, vb))
    return jnp.concatenate(outs, axis=1).astype(v.dtype)


def setup_inputs(seed: int = 0) -> dict:
    key = jax.random.key(seed)
    ks = jax.random.split(key, 26)
    f32 = jnp.float32

    def nrm(k, shape, fan_in):
        return jax.random.normal(k, shape, f32) * (fan_in ** -0.5)

    def gain(k, shape):
        return 1.0 + 0.02 * jax.random.normal(k, shape, f32)

    def bias(k, shape):
        return 0.01 * jax.random.normal(k, shape, f32)

    u = jax.random.uniform(ks[8], (DEPTH, RG_WIDTH), f32, minval=0.9, maxval=0.999)
    a0 = u ** (1.0 / RG_C)
    rg_lambda = jnp.log(a0) - jnp.log1p(-a0)

    return {
        "x": jax.random.normal(ks[0], (BATCH, SEQ, D_MODEL), f32),
        "p": jax.random.normal(ks[1], (DEPTH, BATCH, SEQ, PLE_DIM), f32),
        "g_mix": gain(ks[2], (DEPTH, D_MODEL)),
        "w_in": nrm(ks[3], (DEPTH, D_MODEL, W_IN), D_MODEL),
        "w_rg_conv": nrm(ks[4], (DEPTH, RG_CONV, RG_WIDTH), RG_CONV),
        "b_rg_conv": bias(ks[5], (DEPTH, RG_WIDTH)),
        "w_rg_a": nrm(ks[6], (DEPTH, RG_BLOCKS, RG_BLOCK_DIM, RG_BLOCK_DIM), RG_BLOCK_DIM),
        "b_rg_a": bias(ks[7], (DEPTH, RG_WIDTH)),
        "w_rg_x": nrm(ks[9], (DEPTH, RG_BLOCKS, RG_BLOCK_DIM, RG_BLOCK_DIM), RG_BLOCK_DIM),
        "b_rg_x": bias(ks[10], (DEPTH, RG_WIDTH)),
        "rg_lambda": rg_lambda,
        "w_branch": nrm(ks[11], (DEPTH, N_BRANCH, BRANCH_WIDTH, D_MODEL), BRANCH_WIDTH),
        "w_branch_gate": nrm(ks[12], (DEPTH, D_MODEL, N_BRANCH * D_MODEL), D_MODEL),
        "b_branch_gate": bias(ks[13], (DEPTH, N_BRANCH * D_MODEL)),
        "w_out": nrm(ks[14], (DEPTH, D_MODEL, D_MODEL), D_MODEL),
        "g_ffn": gain(ks[15], (DEPTH, D_MODEL)),
        "w_up": nrm(ks[16], (DEPTH, D_MODEL, 2 * D_FF), D_MODEL),
        "w_ffn_conv": nrm(ks[17], (DEPTH, FFN_CONV, 2 * D_FF), FFN_CONV),
        "b_ffn_conv": bias(ks[18], (DEPTH, 2 * D_FF)),
        "w_down": nrm(ks[19], (DEPTH, D_FF, D_MODEL), D_FF),
        "g_ple": gain(ks[20], (DEPTH, D_MODEL)),
        "w_ple": nrm(ks[21], (DEPTH, PLE_DIM, D_MODEL), PLE_DIM),
        "w_ple_gate": nrm(ks[22], (DEPTH, D_MODEL, D_MODEL), D_MODEL),
        "b_ple_gate": bias(ks[23], (DEPTH, D_MODEL)),
        "g_final": gain(ks[24], (D_MODEL,)),
    }


def reference(x, p, g_mix, w_in, w_rg_conv, b_rg_conv, w_rg_a, b_rg_a, w_rg_x, b_rg_x,
              rg_lambda, w_branch, w_branch_gate, b_branch_gate, w_out, g_ffn, w_up,
              w_ffn_conv, b_ffn_conv, w_down, g_ple, w_ple, w_ple_gate, b_ple_gate, g_final):
    B, S, _ = x.shape
    splits = [RG_WIDTH, 2 * RG_WIDTH, 2 * RG_WIDTH + SB_WIDTH, 2 * RG_WIDTH + 2 * SB_WIDTH]
    for i in range(DEPTH):
        h = rms_norm(x, g_mix[i])
        proj = h @ w_in[i]
        xr, gr, q, k, v = jnp.split(proj, splits, axis=-1)
        xr = causal_dwconv(xr, w_rg_conv[i], b_rg_conv[i])
        y_rec = rg_lru(xr, w_rg_a[i], b_rg_a[i], w_rg_x[i], b_rg_x[i], rg_lambda[i]) * jax.nn.gelu(gr)
        q = q.reshape(B, S, SB_HEADS, SB_HEAD_DIM)
        k = k.reshape(B, S, SB_HEADS, SB_HEAD_DIM)
        v = v.reshape(B, S, SB_HEADS, SB_HEAD_DIM)
        y_att = stick_breaking_attention(q, k, v).reshape(B, S, SB_WIDTH)
        branches = jnp.stack([y_rec, y_att], axis=2)
        branch_d = jnp.einsum("bsnc,ncd->bsnd", branches, w_branch[i])
        gates = jax.nn.sigmoid(h @ w_branch_gate[i] + b_branch_gate[i]).reshape(B, S, N_BRANCH, D_MODEL)
        x = x + jnp.sum(gates * branch_d, axis=2) @ w_out[i]
        h = rms_norm(x, g_ffn[i])
        up = causal_dwconv(h @ w_up[i], w_ffn_conv[i], b_ffn_conv[i])
        a_half, v_half = jnp.split(up, 2, axis=-1)
        x = x + (jax.nn.gelu(a_half) * v_half) @ w_down[i]
        e = p[i].astype(x.dtype) @ w_ple[i]
        g = jax.nn.sigmoid(rms_norm(x, g_ple[i]) @ w_ple_gate[i] + b_ple_gate[i])
        x = x + g * e
    return rms_norm(x, g_final)
```

```python
import functools

import jax
import jax.numpy as jnp
from jax import lax
from jax.experimental import pallas as pl
from jax.experimental.pallas import tpu as pltpu

F32 = jnp.float32
BF16 = jnp.bfloat16

EPS = 1e-6
RG_C = 8.0
SB_HEAD_DIM = 128
SUBLANES = 8
BF16_ROWS = 16
LANES = 128
VMEM_LIMIT = 56 * 1024 * 1024


def _params(sem, vmem=VMEM_LIMIT):
    return pltpu.CompilerParams(dimension_semantics=sem, vmem_limit_bytes=vmem)


def _rms_norm(x, g):
    return x * lax.rsqrt(jnp.mean(x * x, axis=-1, keepdims=True) + EPS) * g


def _dot(a, b):
    return jnp.dot(a, b, preferred_element_type=F32)


def _mix_in_kernel(x_ref, g_ref, w_ref, b_ref, xr_ref, gr_ref, qkv_ref, gate_ref, h_ref,
                   *, j_gr, j_q, j_k, j_gate, q_scale):
    j = pl.program_id(1)

    @pl.when(j == 0)
    def _():
        h_ref[...] = _rms_norm(x_ref[...], g_ref[...]).astype(BF16)

    acc = _dot(h_ref[...], w_ref[...])

    @pl.when(j < j_gr)
    def _():
        xr_ref[...] = acc

    @pl.when((j >= j_gr) & (j < j_q))
    def _():
        gr_ref[...] = jax.nn.gelu(acc).astype(BF16)

    @pl.when((j >= j_q) & (j < j_k))
    def _():
        qkv_ref[...] = (acc * q_scale).astype(BF16)

    @pl.when((j >= j_k) & (j < j_gate))
    def _():
        qkv_ref[...] = acc.astype(BF16)

    @pl.when(j >= j_gate)
    def _():
        gate_ref[...] = jax.nn.sigmoid(acc + b_ref[...]).astype(BF16)


def _mix_in(x, g, w_cat, b_gate, *, rg_width, sb_width, head_dim, tm, tn):
    T, D = x.shape
    n_all = w_cat.shape[1]
    n_gate = b_gate.shape[1]
    j_gr = rg_width // tn
    j_q = 2 * rg_width // tn
    j_k = j_q + sb_width // tn
    j_gate = j_q + 3 * sb_width // tn
    nj = n_all // tn
    assert j_gate * tn + n_gate == n_all and rg_width % tn == 0 and sb_width % tn == 0

    def clamp(j, lo, n):
        return jnp.clip(j - lo, 0, n - 1)

    kern = functools.partial(_mix_in_kernel, j_gr=j_gr, j_q=j_q, j_k=j_k, j_gate=j_gate,
                             q_scale=head_dim ** -0.5)
    return pl.pallas_call(
        kern,
        grid=(T // tm, nj),
        in_specs=[
            pl.BlockSpec((tm, D), lambda i, j: (i, 0)),
            pl.BlockSpec((1, D), lambda i, j: (0, 0)),
            pl.BlockSpec((D, tn), lambda i, j: (0, j)),
            pl.BlockSpec((1, tn), lambda i, j: (0, clamp(j, j_gate, nj - j_gate))),
        ],
        out_specs=[
            pl.BlockSpec((tm, tn), lambda i, j: (i, clamp(j, 0, j_gr))),
            pl.BlockSpec((tm, tn), lambda i, j: (i, clamp(j, j_gr, j_q - j_gr))),
            pl.BlockSpec((tm, tn), lambda i, j: (i, clamp(j, j_q, j_gate - j_q))),
            pl.BlockSpec((tm, tn), lambda i, j: (i, clamp(j, j_gate, nj - j_gate))),
        ],
        out_shape=[
            jax.ShapeDtypeStruct((T, rg_width), F32),
            jax.ShapeDtypeStruct((T, rg_width), BF16),
            jax.ShapeDtypeStruct((T, 3 * sb_width), BF16),
            jax.ShapeDtypeStruct((T, n_gate), BF16),
        ],
        scratch_shapes=[pltpu.VMEM((tm, D), BF16)],
        compiler_params=_params(("arbitrary", "arbitrary")),
        name="mix_in",
    )(x, g, w_cat, b_gate)


def _shift_rows(x, s, fill):
    n = x.shape[0]
    if s % SUBLANES == 0:
        head = jnp.full((s, x.shape[1]), fill, x.dtype)
        return jnp.concatenate([head, x[:n - s]], axis=0)
    rolled = pltpu.roll(x, s, axis=0)
    rows = lax.broadcasted_iota(jnp.int32, (SUBLANES, x.shape[1]), 0)
    head = jnp.where(rows < s, fill, rolled[:SUBLANES])
    return jnp.concatenate([head, rolled[SUBLANES:]], axis=0)


def _rglru_kernel(xr_ref, gr_ref, wc_ref, bc_ref, wax_ref, ba_ref, bx_ref, lam_ref, y_ref,
                  tail_ref, hc_ref, *, n_conv, bd):
    t = pl.program_id(2)

    @pl.when(t == 0)
    def _():
        tail_ref[...] = jnp.zeros_like(tail_ref)
        hc_ref[...] = jnp.zeros_like(hc_ref)

    x = xr_ref[...]
    tc, cw = x.shape
    tail = tail_ref[...]
    wc = wc_ref[...]
    xc = wc[n_conv - 1:n_conv] * x + bc_ref[...]
    for k in range(1, n_conv):
        prev = pltpu.roll(tail, k, axis=0)
        xc = xc + wc[n_conv - 1 - k:n_conv - k] * _shift_rows(x, k, prev)
    tail_ref[...] = x[tc - SUBLANES:]

    xb = xc.astype(BF16)
    r_parts, i_parts = [], []
    for gblk in range(cw // bd):
        ga = _dot(xb[:, gblk * bd:(gblk + 1) * bd], wax_ref[gblk])
        r_parts.append(ga[:, :bd])
        i_parts.append(ga[:, bd:])
    r = jax.nn.sigmoid(jnp.concatenate(r_parts, axis=1) + ba_ref[...])
    ig = jax.nn.sigmoid(jnp.concatenate(i_parts, axis=1) + bx_ref[...])

    nl = -lam_ref[...]
    softplus_nl = jnp.maximum(nl, 0.0) + jnp.log(1.0 + jnp.exp(-jnp.abs(nl)))
    log_a = (-RG_C) * r * softplus_nl
    a = jnp.exp(log_a)
    mult = jnp.sqrt(jnp.tanh(-log_a) * (a * a + 1.0))
    u = xc * ig * mult

    ca, cb = a, u
    s = 1
    while s < tc:
        cb = ca * _shift_rows(cb, s, 0.0) + cb
        ca = ca * _shift_rows(ca, s, 1.0)
        s *= 2
    h = ca * hc_ref[...] + cb
    hc_ref[...] = h[tc - 1:tc]
    y_ref[...] = (h * gr_ref[...].astype(F32)).astype(BF16)


def _rglru(xr, gr, w_conv, b_conv, wax, b_a, b_x, lam, *, tc, cw):
    B, S, C = xr.shape
    n_conv = w_conv.shape[0]
    bd = wax.shape[1]
    row = lambda b, c, t: (0, c)
    act = pl.BlockSpec((None, tc, cw), lambda b, c, t: (b, t, c))
    kern = functools.partial(_rglru_kernel, n_conv=n_conv, bd=bd)
    return pl.pallas_call(
        kern,
        grid=(B, C // cw, S // tc),
        in_specs=[
            act, act,
            pl.BlockSpec((n_conv, cw), row),
            pl.BlockSpec((1, cw), row),
            pl.BlockSpec((cw // bd, bd, 2 * bd), lambda b, c, t: (c, 0, 0)),
            pl.BlockSpec((1, cw), row),
            pl.BlockSpec((1, cw), row),
            pl.BlockSpec((1, cw), row),
        ],
        out_specs=act,
        out_shape=jax.ShapeDtypeStruct((B, S, C), BF16),
        scratch_shapes=[pltpu.VMEM((SUBLANES, cw), F32), pltpu.VMEM((1, cw), F32)],
        compiler_params=_params(("arbitrary", "arbitrary", "arbitrary")),
        name="rglru",
    )(xr, gr, w_conv, b_conv, wax, b_a, b_x, lam)


def _sb_attn_kernel(q_ref, k_ref, v_ref, o_ref, *, tk):
    i = pl.program_id(2)
    q = q_ref[...]
    tq, dh = q.shape
    n_kb = (i + 1) * (tq // tk)
    strict = (lax.broadcasted_iota(jnp.int32, (tk, tk), 0)
              > lax.broadcasted_iota(jnp.int32, (tk, tk), 1)).astype(BF16)
    t_idx = i * tq + lax.broadcasted_iota(jnp.int32, (tq, tk), 0)
    s_off = lax.broadcasted_iota(jnp.int32, (tq, tk), 1)

    def body(n, carry):
        acc, later = carry
        s0 = pl.multiple_of((n_kb - 1 - n) * tk, tk)
        kb = k_ref[pl.ds(s0, tk), :]
        vb = v_ref[pl.ds(s0, tk), :]
        z = lax.dot_general(q, kb, (((1,), (1,)), ((), ())), preferred_element_type=F32)
        mask = (s0 + s_off) < t_idx
        softplus_z = jnp.maximum(z, 0.0) + jnp.log(1.0 + jnp.exp(-jnp.abs(z)))
        log_rest = jnp.where(mask, -softplus_z, 0.0)
        log_beta = z - softplus_z
        hi = log_rest.astype(BF16)
        lo = (log_rest - hi.astype(F32)).astype(BF16)
        suffix = _dot(hi, strict) + _dot(lo, strict)
        att = jnp.where(mask, jnp.exp(log_beta + suffix + later), 0.0)
        acc = acc + _dot(att.astype(BF16), vb)
        later = later + jnp.sum(log_rest, axis=1, keepdims=True)
        return acc, later

    acc, _ = lax.fori_loop(0, n_kb, body, (jnp.zeros((tq, dh), F32), jnp.zeros((tq, 1), F32)))
    o_ref[...] = acc.astype(BF16)


def _sb_attn(qkv, *, n_heads, head_dim, tq, tk):
    B, S, _ = qkv.shape
    H = n_heads
    kern = functools.partial(_sb_attn_kernel, tk=tk)
    return pl.pallas_call(
        kern,
        grid=(B, H, S // tq),
        in_specs=[
            pl.BlockSpec((None, tq, head_dim), lambda b, h, i: (b, i, h)),
            pl.BlockSpec((None, S, head_dim), lambda b, h, i: (b, 0, H + h)),
            pl.BlockSpec((None, S, head_dim), lambda b, h, i: (b, 0, 2 * H + h)),
        ],
        out_specs=pl.BlockSpec((None, tq, head_dim), lambda b, h, i: (b, i, h)),
        out_shape=jax.ShapeDtypeStruct((B, S, H * head_dim), BF16),
        compiler_params=_params(("arbitrary", "arbitrary", "arbitrary")),
        name="sb_attn",
    )(qkv, qkv, qkv)


def _merge_kernel(yr_ref, ya_ref, w0_ref, w1_ref, g0_ref, g1_ref, o_ref):
    d0 = _dot(yr_ref[...], w0_ref[...])
    d1 = _dot(ya_ref[...], w1_ref[...])
    o_ref[...] = (g0_ref[...].astype(F32) * d0 + g1_ref[...].astype(F32) * d1).astype(BF16)


def _merge(y_rec, y_att, w_branch, gates, *, tm, tn):
    T, C = y_rec.shape
    D = w_branch.shape[2]
    nj = D // tn
    return pl.pallas_call(
        _merge_kernel,
        grid=(T // tm, nj),
        in_specs=[
            pl.BlockSpec((tm, C), lambda i, j: (i, 0)),
            pl.BlockSpec((tm, C), lambda i, j: (i, 0)),
            pl.BlockSpec((None, C, tn), lambda i, j: (0, 0, j)),
            pl.BlockSpec((None, C, tn), lambda i, j: (1, 0, j)),
            pl.BlockSpec((tm, tn), lambda i, j: (i, j)),
            pl.BlockSpec((tm, tn), lambda i, j: (i, j + nj)),
        ],
        out_specs=pl.BlockSpec((tm, tn), lambda i, j: (i, j)),
        out_shape=jax.ShapeDtypeStruct((T, D), BF16),
        compiler_params=_params(("arbitrary", "arbitrary")),
        name="merge",
    )(y_rec, y_att, w_branch, w_branch, gates, gates)


def _matmul_res_kernel(a_ref, w_ref, x_ref, o_ref, acc_ref):
    k = pl.program_id(2)

    @pl.when(k == 0)
    def _():
        acc_ref[...] = x_ref[...]

    acc_ref[...] += _dot(a_ref[...], w_ref[...])

    @pl.when(k == pl.num_programs(2) - 1)
    def _():
        o_ref[...] = acc_ref[...]


def _matmul_res_single_kernel(a_ref, w_ref, x_ref, o_ref):
    o_ref[...] = x_ref[...] + _dot(a_ref[...], w_ref[...])


def _matmul_res(a, w, x, *, tm, tn, tk):
    T, K = a.shape
    N = w.shape[1]
    if tk == K:
        return pl.pallas_call(
            _matmul_res_single_kernel,
            grid=(T // tm, N // tn),
            in_specs=[
                pl.BlockSpec((tm, K), lambda i, j: (i, 0)),
                pl.BlockSpec((K, tn), lambda i, j: (0, j)),
                pl.BlockSpec((tm, tn), lambda i, j: (i, j)),
            ],
            out_specs=pl.BlockSpec((tm, tn), lambda i, j: (i, j)),
            out_shape=jax.ShapeDtypeStruct((T, N), F32),
            compiler_params=_params(("arbitrary", "arbitrary")),
            name="matmul_res",
        )(a, w, x)
    return pl.pallas_call(
        _matmul_res_kernel,
        grid=(T // tm, N // tn, K // tk),
        in_specs=[
            pl.BlockSpec((tm, tk), lambda i, j, k: (i, k)),
            pl.BlockSpec((tk, tn), lambda i, j, k: (k, j)),
            pl.BlockSpec((tm, tn), lambda i, j, k: (i, j)),
        ],
        out_specs=pl.BlockSpec((tm, tn), lambda i, j, k: (i, j)),
        out_shape=jax.ShapeDtypeStruct((T, N), F32),
        scratch_shapes=[pltpu.VMEM((tm, tn), F32)],
        compiler_params=_params(("arbitrary", "arbitrary", "arbitrary")),
        name="matmul_res_k",
    )(a, w, x)


def _ffn_up_kernel(x_ref, g_ref, wa_ref, wv_ref, ca_ref, cv_ref, ba_ref, bv_ref, o_ref, h_ref,
                   *, n_conv, blocks_per_seq):
    i = pl.program_id(0)
    j = pl.program_id(1)
    halo = BF16_ROWS
    tm = x_ref.shape[0]

    @pl.when((j == 0) & (i % blocks_per_seq == 0))
    def _():
        h_ref[0:halo, :] = jnp.zeros((halo, h_ref.shape[1]), BF16)

    @pl.when((j == 0) & (i % blocks_per_seq != 0))
    def _():
        h_ref[0:halo, :] = h_ref[tm:tm + halo, :]

    @pl.when(j == 0)
    def _():
        h_ref[halo:, :] = _rms_norm(x_ref[...], g_ref[...]).astype(BF16)

    h = h_ref[...]

    def conv(w_ref, c_ref, b_ref):
        up = _dot(h, w_ref[...])
        c = c_ref[...]
        out = c[n_conv - 1:n_conv] * up
        for k in range(1, n_conv):
            out = out + c[n_conv - 1 - k:n_conv - k] * pltpu.roll(up, k, axis=0)
        return out[halo:] + b_ref[...]

    a_half = conv(wa_ref, ca_ref, ba_ref)
    v_half = conv(wv_ref, cv_ref, bv_ref)
    o_ref[...] = (jax.nn.gelu(a_half) * v_half).astype(BF16)


def _ffn_up(x, g, w_up, w_conv, b_conv, *, seq, tm, tn):
    T, D = x.shape
    F = w_up.shape[1] // 2
    nj = F // tn
    n_conv = w_conv.shape[0]
    assert seq % tm == 0
    kern = functools.partial(_ffn_up_kernel, n_conv=n_conv, blocks_per_seq=seq // tm)
    lo = lambda i, j: (0, j)
    hi = lambda i, j: (0, j + nj)
    return pl.pallas_call(
        kern,
        grid=(T // tm, nj),
        in_specs=[
            pl.BlockSpec((tm, D), lambda i, j: (i, 0)),
            pl.BlockSpec((1, D), lambda i, j: (0, 0)),
            pl.BlockSpec((D, tn), lo),
            pl.BlockSpec((D, tn), hi),
            pl.BlockSpec((n_conv, tn), lo),
            pl.BlockSpec((n_conv, tn), hi),
            pl.BlockSpec((1, tn), lo),
            pl.BlockSpec((1, tn), hi),
        ],
        out_specs=pl.BlockSpec((tm, tn), lambda i, j: (i, j)),
        out_shape=jax.ShapeDtypeStruct((T, F), BF16),
        scratch_shapes=[pltpu.VMEM((tm + BF16_ROWS, D), BF16)],
        compiler_params=_params(("arbitrary", "arbitrary")),
        name="ffn_up",
    )(x, g, w_up, w_up, w_conv, w_conv, b_conv, b_conv)


def _ple_kernel(x_ref, xt_ref, g_ref, p_ref, wp_ref, wg_ref, b_ref, o_ref, h_ref):
    @pl.when(pl.program_id(1) == 0)
    def _():
        h_ref[...] = _rms_norm(x_ref[...], g_ref[...]).astype(BF16)

    e = _dot(p_ref[...].astype(BF16), wp_ref[...])
    gate = jax.nn.sigmoid(_dot(h_ref[...], wg_ref[...]) + b_ref[...])
    o_ref[...] = xt_ref[...] + gate * e


def _ple(x, g, p, w_ple, w_gate, b_gate, *, tm, tn):
    T, D = x.shape
    P = p.shape[1]
    return pl.pallas_call(
        _ple_kernel,
        grid=(T // tm, D // tn),
        in_specs=[
            pl.BlockSpec((tm, D), lambda i, j: (i, 0)),
            pl.BlockSpec((tm, tn), lambda i, j: (i, j)),
            pl.BlockSpec((1, D), lambda i, j: (0, 0)),
            pl.BlockSpec((tm, P), lambda i, j: (i, 0)),
            pl.BlockSpec((P, tn), lambda i, j: (0, j)),
            pl.BlockSpec((D, tn), lambda i, j: (0, j)),
            pl.BlockSpec((1, tn), lambda i, j: (0, j)),
        ],
        out_specs=pl.BlockSpec((tm, tn), lambda i, j: (i, j)),
        out_shape=jax.ShapeDtypeStruct((T, D), F32),
        scratch_shapes=[pltpu.VMEM((tm, D), BF16)],
        compiler_params=_params(("arbitrary", "arbitrary")),
        name="ple",
    )(x, x, g, p, w_ple, w_gate, b_gate)


def _norm_kernel(x_ref, g_ref, o_ref):
    o_ref[...] = _rms_norm(x_ref[...], g_ref[...])


def _final_norm(x, g, *, tm):
    T, D = x.shape
    return pl.pallas_call(
        _norm_kernel,
        grid=(T // tm,),
        in_specs=[pl.BlockSpec((tm, D), lambda i: (i, 0)), pl.BlockSpec((1, D), lambda i: (0, 0))],
        out_specs=pl.BlockSpec((tm, D), lambda i: (i, 0)),
        out_shape=jax.ShapeDtypeStruct((T, D), F32),
        compiler_params=_params(("arbitrary",)),
        name="final_norm",
    )(x, g)


def _tile(n, want):
    t = min(n, want)
    while n % t:
        t //= 2
    return t


def kernel(x, p, g_mix, w_in, w_rg_conv, b_rg_conv, w_rg_a, b_rg_a, w_rg_x, b_rg_x, rg_lambda,
           w_branch, w_branch_gate, b_branch_gate, w_out, g_ffn, w_up, w_ffn_conv, b_ffn_conv,
           w_down, g_ple, w_ple, w_ple_gate, b_ple_gate, g_final):
    B, S, D = x.shape
    T = B * S
    depth = w_in.shape[0]
    rg_width = w_rg_conv.shape[2]
    n_rg_blocks, bd = w_rg_a.shape[1], w_rg_a.shape[2]
    sb_width = (w_in.shape[2] - 2 * rg_width) // 3
    head_dim = SB_HEAD_DIM
    n_heads = sb_width // head_dim
    d_ff = w_down.shape[1]

    tm = _tile(S, 512)
    row = lambda a: a.reshape(1, -1)

    xf = x.reshape(T, D)
    for l in range(depth):
        w_cat = jnp.concatenate([w_in[l], w_branch_gate[l]], axis=1).astype(BF16)
        xr, gr, qkv, gates = _mix_in(
            xf, row(g_mix[l]), w_cat, row(b_branch_gate[l]),
            rg_width=rg_width, sb_width=sb_width, head_dim=head_dim, tm=tm, tn=_tile(rg_width, 512))

        wax = jnp.concatenate([w_rg_a[l], w_rg_x[l]], axis=2).astype(BF16)
        y_rec = _rglru(
            xr.reshape(B, S, rg_width), gr.reshape(B, S, rg_width), w_rg_conv[l], row(b_rg_conv[l]),
            wax, row(b_rg_a[l]), row(b_rg_x[l]), row(rg_lambda[l]),
            tc=_tile(S, 256), cw=_tile(rg_width, 512))
        y_att = _sb_attn(qkv.reshape(B, S, 3 * sb_width), n_heads=n_heads, head_dim=head_dim,
                         tq=_tile(S, 256), tk=_tile(S, 256))

        merged = _merge(y_rec.reshape(T, rg_width), y_att.reshape(T, sb_width),
                        w_branch[l].astype(BF16), gates, tm=_tile(T, 1024), tn=_tile(D, 512))
        xf = _matmul_res(merged, w_out[l].astype(BF16), xf, tm=_tile(T, 1024), tn=_tile(D, 512), tk=D)

        act = _ffn_up(xf, row(g_ffn[l]), w_up[l].astype(BF16), w_ffn_conv[l], row(b_ffn_conv[l]),
                      seq=S, tm=tm, tn=_tile(d_ff, 512))
        xf = _matmul_res(act, w_down[l].astype(BF16), xf,
                         tm=_tile(T, 1024), tn=_tile(D, 1024), tk=_tile(d_ff, 2048))

        xf = _ple(xf, row(g_ple[l]), p[l].reshape(T, -1), w_ple[l].astype(BF16),
                  w_ple_gate[l].astype(BF16), row(b_ple_gate[l]), tm=tm, tn=_tile(D, 512))

    return _final_norm(xf, row(g_final), tm=_tile(T, 256)).reshape(B, S, D)
```

```python
import functools

import jax
import jax.numpy as jnp
from jax import lax
from jax.experimental import pallas as pl
from jax.experimental.pallas import tpu as pltpu

F32 = jnp.float32
BF16 = jnp.bfloat16

EPS = 1e-6
RG_C = 8.0
LOG2E = 1.4426950408889634
SB_HEAD_DIM = 128
SUBLANES = 8
BF16_ROWS = 16
LANES = 128
VMEM_LIMIT = 56 * 1024 * 1024


def _params(sem, vmem=VMEM_LIMIT):
    return pltpu.CompilerParams(dimension_semantics=sem, vmem_limit_bytes=vmem)


def _rms_norm(x, g):
    return x * lax.rsqrt(jnp.mean(x * x, axis=-1, keepdims=True) + EPS) * g


def _dot(a, b):
    return jnp.dot(a, b, preferred_element_type=F32)


def _mix_in_kernel(x_ref, g_ref, w_ref, b_ref, xr_ref, gr_ref, qkv_ref, gate_ref, h_ref,
                   *, j_gr, j_q, j_k, j_gate, q_scale):
    j = pl.program_id(1)

    @pl.when(j == 0)
    def _():
        h_ref[...] = _rms_norm(x_ref[...], g_ref[...]).astype(BF16)

    acc = _dot(h_ref[...], w_ref[...])

    @pl.when(j < j_gr)
    def _():
        xr_ref[...] = acc

    @pl.when((j >= j_gr) & (j < j_q))
    def _():
        gr_ref[...] = jax.nn.gelu(acc).astype(BF16)

    @pl.when((j >= j_q) & (j < j_k))
    def _():
        qkv_ref[...] = (acc * q_scale).astype(BF16)

    @pl.when((j >= j_k) & (j < j_gate))
    def _():
        qkv_ref[...] = acc.astype(BF16)

    @pl.when(j >= j_gate)
    def _():
        gate_ref[...] = jax.nn.sigmoid(acc + b_ref[...]).astype(BF16)


def _mix_in(x, g, w_cat, b_gate, *, rg_width, sb_width, head_dim, tm, tn):
    T, D = x.shape
    n_all = w_cat.shape[1]
    n_gate = b_gate.shape[1]
    j_gr = rg_width // tn
    j_q = 2 * rg_width // tn
    j_k = j_q + sb_width // tn
    j_gate = j_q + 3 * sb_width // tn
    nj = n_all // tn
    assert j_gate * tn + n_gate == n_all and rg_width % tn == 0 and sb_width % tn == 0

    def clamp(j, lo, n):
        return jnp.clip(j - lo, 0, n - 1)

    kern = functools.partial(_mix_in_kernel, j_gr=j_gr, j_q=j_q, j_k=j_k, j_gate=j_gate,
                             q_scale=LOG2E * head_dim ** -0.5)
    return pl.pallas_call(
        kern,
        grid=(T // tm, nj),
        in_specs=[
            pl.BlockSpec((tm, D), lambda i, j: (i, 0)),
            pl.BlockSpec((1, D), lambda i, j: (0, 0)),
            pl.BlockSpec((D, tn), lambda i, j: (0, j)),
            pl.BlockSpec((1, tn), lambda i, j: (0, clamp(j, j_gate, nj - j_gate))),
        ],
        out_specs=[
            pl.BlockSpec((tm, tn), lambda i, j: (i, clamp(j, 0, j_gr))),
            pl.BlockSpec((tm, tn), lambda i, j: (i, clamp(j, j_gr, j_q - j_gr))),
            pl.BlockSpec((tm, tn), lambda i, j: (i, clamp(j, j_q, j_gate - j_q))),
            pl.BlockSpec((tm, tn), lambda i, j: (i, clamp(j, j_gate, nj - j_gate))),
        ],
        out_shape=[
            jax.ShapeDtypeStruct((T, rg_width), F32),
            jax.ShapeDtypeStruct((T, rg_width), BF16),
            jax.ShapeDtypeStruct((T, 3 * sb_width), BF16),
            jax.ShapeDtypeStruct((T, n_gate), BF16),
        ],
        scratch_shapes=[pltpu.VMEM((tm, D), BF16)],
        compiler_params=_params(("arbitrary", "arbitrary")),
        name="mix_in",
    )(x, g, w_cat, b_gate)


def _shift_rows(x, s, fill):
    n = x.shape[0]
    if s % SUBLANES == 0:
        head = jnp.full((s, x.shape[1]), fill, x.dtype)
        return jnp.concatenate([head, x[:n - s]], axis=0)
    rolled = pltpu.roll(x, s, axis=0)
    rows = lax.broadcasted_iota(jnp.int32, (SUBLANES, x.shape[1]), 0)
    head = jnp.where(rows < s, fill, rolled[:SUBLANES])
    return jnp.concatenate([head, rolled[SUBLANES:]], axis=0)


def _rglru_kernel(xr_ref, gr_ref, wc_ref, bc_ref, wax_ref, ba_ref, bx_ref, lam_ref, y_ref,
                  tail_ref, hc_ref, *, n_conv, bd):
    t = pl.program_id(2)

    @pl.when(t == 0)
    def _():
        tail_ref[...] = jnp.zeros_like(tail_ref)
        hc_ref[...] = jnp.zeros_like(hc_ref)

    x = xr_ref[...]
    tc, cw = x.shape
    tail = tail_ref[...]
    wc = wc_ref[...]
    xc = wc[n_conv - 1:n_conv] * x + bc_ref[...]
    for k in range(1, n_conv):
        prev = pltpu.roll(tail, k, axis=0)
        xc = xc + wc[n_conv - 1 - k:n_conv - k] * _shift_rows(x, k, prev)
    tail_ref[...] = x[tc - SUBLANES:]

    xb = xc.astype(BF16)
    r_parts, i_parts = [], []
    for gblk in range(cw // bd):
        ga = _dot(xb[:, gblk * bd:(gblk + 1) * bd], wax_ref[gblk])
        r_parts.append(ga[:, :bd])
        i_parts.append(ga[:, bd:])
    r = jax.nn.sigmoid(jnp.concatenate(r_parts, axis=1) + ba_ref[...])
    ig = jax.nn.sigmoid(jnp.concatenate(i_parts, axis=1) + bx_ref[...])

    nl = -lam_ref[...]
    softplus_nl = jnp.maximum(nl, 0.0) + jnp.log(1.0 + jnp.exp(-jnp.abs(nl)))
    log_a = (-RG_C) * r * softplus_nl
    a = jnp.exp(log_a)
    mult = jnp.sqrt(jnp.tanh(-log_a) * (a * a + 1.0))
    u = xc * ig * mult

    ca, cb = a, u
    s = 1
    while s < tc:
        cb = ca * _shift_rows(cb, s, 0.0) + cb
        ca = ca * _shift_rows(ca, s, 1.0)
        s *= 2
    h = ca * hc_ref[...] + cb
    hc_ref[...] = h[tc - 1:tc]
    y_ref[...] = (h * gr_ref[...].astype(F32)).astype(BF16)


def _rglru(xr, gr, w_conv, b_conv, wax, b_a, b_x, lam, *, tc, cw):
    B, S, C = xr.shape
    n_conv = w_conv.shape[0]
    bd = wax.shape[1]
    row = lambda b, c, t: (0, c)
    act = pl.BlockSpec((None, tc, cw), lambda b, c, t: (b, t, c))
    kern = functools.partial(_rglru_kernel, n_conv=n_conv, bd=bd)
    return pl.pallas_call(
        kern,
        grid=(B, C // cw, S // tc),
        in_specs=[
            act, act,
            pl.BlockSpec((n_conv, cw), row),
            pl.BlockSpec((1, cw), row),
            pl.BlockSpec((cw // bd, bd, 2 * bd), lambda b, c, t: (c, 0, 0)),
            pl.BlockSpec((1, cw), row),
            pl.BlockSpec((1, cw), row),
            pl.BlockSpec((1, cw), row),
        ],
        out_specs=act,
        out_shape=jax.ShapeDtypeStruct((B, S, C), BF16),
        scratch_shapes=[pltpu.VMEM((SUBLANES, cw), F32), pltpu.VMEM((1, cw), F32)],
        compiler_params=_params(("arbitrary", "arbitrary", "arbitrary")),
        name="rglru",
    )(xr, gr, w_conv, b_conv, wax, b_a, b_x, lam)


def _sb_scores(z, mask):
    neg_abs = lax.bitcast_convert_type(lax.bitcast_convert_type(z, jnp.uint32) | jnp.uint32(0x80000000), F32)
    softplus = jnp.maximum(z, 0.0) + jnp.log(1.0 + jnp.exp2(neg_abs)) * LOG2E
    log_beta = z - softplus
    if mask is not None:
        softplus = jnp.where(mask, softplus, 0.0)
    hi = softplus.astype(BF16)
    lo = (softplus - hi.astype(F32)).astype(BF16)
    return log_beta, jnp.concatenate([hi, lo], axis=1), jnp.sum(softplus, axis=1, keepdims=True)


def _sb_weights(log_beta, suffix, later, mask):
    att = jnp.exp2(log_beta - suffix - later)
    if mask is not None:
        att = jnp.where(mask, att, 0.0)
    return att.astype(BF16)


def _sb_attn_kernel(q_ref, k_ref, v_ref, o_ref, *, dh):
    i = pl.program_id(2)
    tq = q_ref.shape[0]
    tk = tq
    n_heads = q_ref.shape[1] // dh
    rows = lax.broadcasted_iota(jnp.int32, (tk, tk), 0)
    cols = lax.broadcasted_iota(jnp.int32, (tk, tk), 1)
    strict = (rows > cols).astype(BF16)
    strict2 = jnp.concatenate([strict, strict], axis=0)
    causal = cols < rows

    def sweep(s0, carry, mask):
        heads = range(n_heads)
        lanes = [slice(h * dh, (h + 1) * dh) for h in heads]
        zs = [lax.dot_general(q_ref[:, lanes[h]], k_ref[pl.ds(s0, tk), lanes[h]],
                              (((1,), (1,)), ((), ())), preferred_element_type=F32) for h in heads]
        scores = [_sb_scores(zs[h], mask) for h in heads]
        suffixes = [_dot(scores[h][1], strict2) for h in heads]
        atts = [_sb_weights(scores[h][0], suffixes[h], carry[h][1], mask) for h in heads]
        pvs = [_dot(atts[h], v_ref[pl.ds(s0, tk), lanes[h]]) for h in heads]
        return tuple((carry[h][0] + pvs[h], carry[h][1] + scores[h][2]) for h in heads)

    carry = tuple((jnp.zeros((tq, dh), F32), jnp.zeros((tq, 1), F32)) for _ in range(n_heads))
    carry = sweep(pl.multiple_of(i * tq, tq), carry, causal)
    carry = lax.fori_loop(0, i, lambda n, c: sweep(pl.multiple_of((i - 1 - n) * tk, tk), c, None), carry)
    for h in range(n_heads):
        o_ref[:, h * dh:(h + 1) * dh] = carry[h][0].astype(BF16)


def _sb_attn(qkv, *, n_heads, head_dim, tq, heads_per_step):
    B, S, _ = qkv.shape
    hw = heads_per_step * head_dim
    n_groups = n_heads // heads_per_step
    kern = functools.partial(_sb_attn_kernel, dh=head_dim)
    return pl.pallas_call(
        kern,
        grid=(B, n_groups, S // tq),
        in_specs=[
            pl.BlockSpec((None, tq, hw), lambda b, h, i: (b, i, h)),
            pl.BlockSpec((None, S, hw), lambda b, h, i: (b, 0, n_groups + h)),
            pl.BlockSpec((None, S, hw), lambda b, h, i: (b, 0, 2 * n_groups + h)),
        ],
        out_specs=pl.BlockSpec((None, tq, hw), lambda b, h, i: (b, i, h)),
        out_shape=jax.ShapeDtypeStruct((B, S, n_heads * head_dim), BF16),
        compiler_params=_params(("arbitrary", "arbitrary", "arbitrary")),
        name="sb_attn",
    )(qkv, qkv, qkv)


def _merge_kernel(yr_ref, ya_ref, w0_ref, w1_ref, g0_ref, g1_ref, o_ref):
    d0 = _dot(yr_ref[...], w0_ref[...])
    d1 = _dot(ya_ref[...], w1_ref[...])
    o_ref[...] = (g0_ref[...].astype(F32) * d0 + g1_ref[...].astype(F32) * d1).astype(BF16)


def _merge(y_rec, y_att, w_branch, gates, *, tm, tn):
    T, C = y_rec.shape
    D = w_branch.shape[2]
    nj = D // tn
    return pl.pallas_call(
        _merge_kernel,
        grid=(T // tm, nj),
        in_specs=[
            pl.BlockSpec((tm, C), lambda i, j: (i, 0)),
            pl.BlockSpec((tm, C), lambda i, j: (i, 0)),
            pl.BlockSpec((None, C, tn), lambda i, j: (0, 0, j)),
            pl.BlockSpec((None, C, tn), lambda i, j: (1, 0, j)),
            pl.BlockSpec((tm, tn), lambda i, j: (i, j)),
            pl.BlockSpec((tm, tn), lambda i, j: (i, j + nj)),
        ],
        out_specs=pl.BlockSpec((tm, tn), lambda i, j: (i, j)),
        out_shape=jax.ShapeDtypeStruct((T, D), BF16),
        compiler_params=_params(("arbitrary", "arbitrary")),
        name="merge",
    )(y_rec, y_att, w_branch, w_branch, gates, gates)


def _matmul_res_kernel(a_ref, w_ref, x_ref, o_ref, acc_ref):
    k = pl.program_id(2)

    @pl.when(k == 0)
    def _():
        acc_ref[...] = x_ref[...]

    acc_ref[...] += _dot(a_ref[...], w_ref[...])

    @pl.when(k == pl.num_programs(2) - 1)
    def _():
        o_ref[...] = acc_ref[...]


def _matmul_res_single_kernel(a_ref, w_ref, x_ref, o_ref):
    o_ref[...] = x_ref[...] + _dot(a_ref[...], w_ref[...])


def _matmul_res(a, w, x, *, tm, tn, tk):
    T, K = a.shape
    N = w.shape[1]
    if tk == K:
        return pl.pallas_call(
            _matmul_res_single_kernel,
            grid=(T // tm, N // tn),
            in_specs=[
                pl.BlockSpec((tm, K), lambda i, j: (i, 0)),
                pl.BlockSpec((K, tn), lambda i, j: (0, j)),
                pl.BlockSpec((tm, tn), lambda i, j: (i, j)),
            ],
            out_specs=pl.BlockSpec((tm, tn), lambda i, j: (i, j)),
            out_shape=jax.ShapeDtypeStruct((T, N), F32),
            compiler_params=_params(("arbitrary", "arbitrary")),
            name="matmul_res",
        )(a, w, x)
    return pl.pallas_call(
        _matmul_res_kernel,
        grid=(T // tm, N // tn, K // tk),
        in_specs=[
            pl.BlockSpec((tm, tk), lambda i, j, k: (i, k)),
            pl.BlockSpec((tk, tn), lambda i, j, k: (k, j)),
            pl.BlockSpec((tm, tn), lambda i, j, k: (i, j)),
        ],
        out_specs=pl.BlockSpec((tm, tn), lambda i, j, k: (i, j)),
        out_shape=jax.ShapeDtypeStruct((T, N), F32),
        scratch_shapes=[pltpu.VMEM((tm, tn), F32)],
        compiler_params=_params(("arbitrary", "arbitrary", "arbitrary")),
        name="matmul_res_k",
    )(a, w, x)


def _ffn_up_kernel(x_ref, g_ref, wa_ref, wv_ref, ca_ref, cv_ref, ba_ref, bv_ref, o_ref, h_ref,
                   *, n_conv, blocks_per_seq):
    i = pl.program_id(0)
    j = pl.program_id(1)
    halo = BF16_ROWS
    tm = x_ref.shape[0]

    @pl.when((j == 0) & (i % blocks_per_seq == 0))
    def _():
        h_ref[0:halo, :] = jnp.zeros((halo, h_ref.shape[1]), BF16)

    @pl.when((j == 0) & (i % blocks_per_seq != 0))
    def _():
        h_ref[0:halo, :] = h_ref[tm:tm + halo, :]

    @pl.when(j == 0)
    def _():
        h_ref[halo:, :] = _rms_norm(x_ref[...], g_ref[...]).astype(BF16)

    h = h_ref[...]

    def conv(w_ref, c_ref, b_ref):
        up = _dot(h, w_ref[...])
        c = c_ref[...]
        out = c[n_conv - 1:n_conv] * up
        for k in range(1, n_conv):
            out = out + c[n_conv - 1 - k:n_conv - k] * pltpu.roll(up, k, axis=0)
        return out[halo:] + b_ref[...]

    a_half = conv(wa_ref, ca_ref, ba_ref)
    v_half = conv(wv_ref, cv_ref, bv_ref)
    o_ref[...] = (jax.nn.gelu(a_half) * v_half).astype(BF16)


def _ffn_up(x, g, w_up, w_conv, b_conv, *, seq, tm, tn):
    T, D = x.shape
    F = w_up.shape[1] // 2
    nj = F // tn
    n_conv = w_conv.shape[0]
    assert seq % tm == 0
    kern = functools.partial(_ffn_up_kernel, n_conv=n_conv, blocks_per_seq=seq // tm)
    lo = lambda i, j: (0, j)
    hi = lambda i, j: (0, j + nj)
    return pl.pallas_call(
        kern,
        grid=(T // tm, nj),
        in_specs=[
            pl.BlockSpec((tm, D), lambda i, j: (i, 0)),
            pl.BlockSpec((1, D), lambda i, j: (0, 0)),
            pl.BlockSpec((D, tn), lo),
            pl.BlockSpec((D, tn), hi),
            pl.BlockSpec((n_conv, tn), lo),
            pl.BlockSpec((n_conv, tn), hi),
            pl.BlockSpec((1, tn), lo),
            pl.BlockSpec((1, tn), hi),
        ],
        out_specs=pl.BlockSpec((tm, tn), lambda i, j: (i, j)),
        out_shape=jax.ShapeDtypeStruct((T, F), BF16),
        scratch_shapes=[pltpu.VMEM((tm + BF16_ROWS, D), BF16)],
        compiler_params=_params(("arbitrary", "arbitrary")),
        name="ffn_up",
    )(x, g, w_up, w_up, w_conv, w_conv, b_conv, b_conv)


def _ple_kernel(x_ref, xt_ref, g_ref, p_ref, wp_ref, wg_ref, b_ref, o_ref, h_ref):
    @pl.when(pl.program_id(1) == 0)
    def _():
        h_ref[...] = _rms_norm(x_ref[...], g_ref[...]).astype(BF16)

    e = _dot(p_ref[...].astype(BF16), wp_ref[...])
    gate = jax.nn.sigmoid(_dot(h_ref[...], wg_ref[...]) + b_ref[...])
    o_ref[...] = xt_ref[...] + gate * e


def _ple(x, g, p, w_ple, w_gate, b_gate, *, tm, tn):
    T, D = x.shape
    P = p.shape[1]
    return pl.pallas_call(
        _ple_kernel,
        grid=(T // tm, D // tn),
        in_specs=[
            pl.BlockSpec((tm, D), lambda i, j: (i, 0)),
            pl.BlockSpec((tm, tn), lambda i, j: (i, j)),
            pl.BlockSpec((1, D), lambda i, j: (0, 0)),
            pl.BlockSpec((tm, P), lambda i, j: (i, 0)),
            pl.BlockSpec((P, tn), lambda i, j: (0, j)),
            pl.BlockSpec((D, tn), lambda i, j: (0, j)),
            pl.BlockSpec((1, tn), lambda i, j: (0, j)),
        ],
        out_specs=pl.BlockSpec((tm, tn), lambda i, j: (i, j)),
        out_shape=jax.ShapeDtypeStruct((T, D), F32),
        scratch_shapes=[pltpu.VMEM((tm, D), BF16)],
        compiler_params=_params(("arbitrary", "arbitrary")),
        name="ple",
    )(x, x, g, p, w_ple, w_gate, b_gate)


def _norm_kernel(x_ref, g_ref, o_ref):
    o_ref[...] = _rms_norm(x_ref[...], g_ref[...])


def _final_norm(x, g, *, tm):
    T, D = x.shape
    return pl.pallas_call(
        _norm_kernel,
        grid=(T // tm,),
        in_specs=[pl.BlockSpec((tm, D), lambda i: (i, 0)), pl.BlockSpec((1, D), lambda i: (0, 0))],
        out_specs=pl.BlockSpec((tm, D), lambda i: (i, 0)),
        out_shape=jax.ShapeDtypeStruct((T, D), F32),
        compiler_params=_params(("arbitrary",)),
        name="final_norm",
    )(x, g)


def _tile(n, want):
    t = min(n, want)
    while n % t:
        t //= 2
    return t


def kernel(x, p, g_mix, w_in, w_rg_conv, b_rg_conv, w_rg_a, b_rg_a, w_rg_x, b_rg_x, rg_lambda,
           w_branch, w_branch_gate, b_branch_gate, w_out, g_ffn, w_up, w_ffn_conv, b_ffn_conv,
           w_down, g_ple, w_ple, w_ple_gate, b_ple_gate, g_final):
    B, S, D = x.shape
    T = B * S
    depth = w_in.shape[0]
    rg_width = w_rg_conv.shape[2]
    n_rg_blocks, bd = w_rg_a.shape[1], w_rg_a.shape[2]
    sb_width = (w_in.shape[2] - 2 * rg_width) // 3
    head_dim = SB_HEAD_DIM
    n_heads = sb_width // head_dim
    d_ff = w_down.shape[1]

    tm = _tile(S, 512)
    row = lambda a: a.reshape(1, -1)

    xf = x.reshape(T, D)
    for l in range(depth):
        w_cat = jnp.concatenate([w_in[l], w_branch_gate[l]], axis=1).astype(BF16)
        xr, gr, qkv, gates = _mix_in(
            xf, row(g_mix[l]), w_cat, row(b_branch_gate[l]),
            rg_width=rg_width, sb_width=sb_width, head_dim=head_dim, tm=tm, tn=_tile(rg_width, 512))

        wax = jnp.concatenate([w_rg_a[l], w_rg_x[l]], axis=2).astype(BF16)
        y_rec = _rglru(
            xr.reshape(B, S, rg_width), gr.reshape(B, S, rg_width), w_rg_conv[l], row(b_rg_conv[l]),
            wax, row(b_rg_a[l]), row(b_rg_x[l]), row(rg_lambda[l]),
            tc=_tile(S, 256), cw=_tile(rg_width, 512))
        y_att = _sb_attn(qkv.reshape(B, S, 3 * sb_width), n_heads=n_heads, head_dim=head_dim,
                         tq=_tile(S, 256), heads_per_step=min(4, n_heads))

        merged = _merge(y_rec.reshape(T, rg_width), y_att.reshape(T, sb_width),
                        w_branch[l].astype(BF16), gates, tm=_tile(T, 1024), tn=_tile(D, 512))
        xf = _matmul_res(merged, w_out[l].astype(BF16), xf, tm=_tile(T, 1024), tn=_tile(D, 512), tk=D)

        act = _ffn_up(xf, row(g_ffn[l]), w_up[l].astype(BF16), w_ffn_conv[l], row(b_ffn_conv[l]),
                      seq=S, tm=tm, tn=_tile(d_ff, 512))
        xf = _matmul_res(act, w_down[l].astype(BF16), xf,
                         tm=_tile(T, 1024), tn=_tile(D, 1024), tk=_tile(d_ff, 2048))

        xf = _ple(xf, row(g_ple[l]), p[l].reshape(T, -1), w_ple[l].astype(BF16),
                  w_ple_gate[l].astype(BF16), row(b_ple_gate[l]), tm=tm, tn=_tile(D, 512))

    return _final_norm(xf, row(g_final), tm=_tile(T, 256)).reshape(B, S, D)
```

```python
import functools

import jax
import jax.numpy as jnp
from jax import lax
from jax.experimental import pallas as pl
from jax.experimental.pallas import tpu as pltpu

F32 = jnp.float32
BF16 = jnp.bfloat16

EPS = 1e-6
RG_C = 8.0
LOG2E = 1.4426950408889634
SB_HEAD_DIM = 128
SUBLANES = 8
MXU_COLS = 256
VMEM_LIMIT = 56 * 1024 * 1024


def _params(sem):
    return pltpu.CompilerParams(dimension_semantics=sem, vmem_limit_bytes=VMEM_LIMIT)


def _dot(a, b):
    return jnp.dot(a, b, preferred_element_type=F32)


def _sum_sq(x):
    return jnp.sum(x * x, axis=1, keepdims=True)


def _cast_kernel(w_ref, o_ref):
    o_ref[...] = w_ref[...].astype(BF16)


def _cast_layer(w, layer, *, tr=512, tc=2048):
    _, R, C = w.shape
    tr, tc = _tile(R, tr), _tile(C, tc)
    return pl.pallas_call(
        _cast_kernel,
        grid=(R // tr, C // tc),
        in_specs=[pl.BlockSpec((None, tr, tc), lambda i, j: (layer, i, j))],
        out_specs=pl.BlockSpec((tr, tc), lambda i, j: (i, j)),
        out_shape=jax.ShapeDtypeStruct((R, C), BF16),
        compiler_params=_params(("arbitrary", "arbitrary")),
        name="cast_w",
    )(w)


def _prep_kernel(x_ref, g_ref, xg_ref, r_ref):
    x = x_ref[...]
    xg_ref[...] = (x * g_ref[...]).astype(BF16)
    r_ref[...] = lax.rsqrt(_sum_sq(x) * (1.0 / x.shape[1]) + EPS)


def _prep(x, g, *, tm):
    T, D = x.shape
    return pl.pallas_call(
        _prep_kernel,
        grid=(T // tm,),
        in_specs=[pl.BlockSpec((tm, D), lambda i: (i, 0)), pl.BlockSpec((1, D), lambda i: (0, 0))],
        out_specs=[pl.BlockSpec((tm, D), lambda i: (i, 0)), pl.BlockSpec((tm, 1), lambda i: (i, 0))],
        out_shape=[jax.ShapeDtypeStruct((T, D), BF16), jax.ShapeDtypeStruct((T, 1), F32)],
        compiler_params=_params(("arbitrary",)),
        name="prep",
    )(x, g)


def _mix_in_kernel(xg_ref, r_ref, win_ref, wbg_ref, b_ref, xr_ref, gr_ref, qkv_ref, gate_ref,
                   *, j_gr, j_q, j_k, j_gate, q_scale):
    j = pl.program_id(1)

    @pl.when(j < j_gate)
    def _():
        acc = _dot(xg_ref[...], win_ref[...]) * r_ref[...]

        @pl.when(j < j_gr)
        def _():
            xr_ref[...] = acc

        @pl.when((j >= j_gr) & (j < j_q))
        def _():
            gr_ref[...] = jax.nn.gelu(acc).astype(BF16)

        @pl.when((j >= j_q) & (j < j_k))
        def _():
            qkv_ref[...] = (acc * q_scale).astype(BF16)

        @pl.when(j >= j_k)
        def _():
            qkv_ref[...] = acc.astype(BF16)

    @pl.when(j >= j_gate)
    def _():
        acc = _dot(xg_ref[...], wbg_ref[...]) * r_ref[...]
        gate_ref[...] = jax.nn.sigmoid(acc + b_ref[...]).astype(BF16)


def _mix_in(xg, r, w_in, w_bg, b_gate, *, rg_width, sb_width, head_dim, tm, tn):
    T, D = xg.shape
    n_in, n_gate = w_in.shape[1], w_bg.shape[1]
    j_gr = rg_width // tn
    j_q = 2 * rg_width // tn
    j_k = j_q + sb_width // tn
    j_gate = n_in // tn
    nj = j_gate + n_gate // tn
    assert rg_width % tn == 0 and sb_width % tn == 0 and n_in == 2 * rg_width + 3 * sb_width

    def clamp(j, lo, n):
        return jnp.clip(j - lo, 0, n - 1)

    kern = functools.partial(_mix_in_kernel, j_gr=j_gr, j_q=j_q, j_k=j_k, j_gate=j_gate,
                             q_scale=LOG2E * head_dim ** -0.5)
    return pl.pallas_call(
        kern,
        grid=(T // tm, nj),
        in_specs=[
            pl.BlockSpec((tm, D), lambda i, j: (i, 0)),
            pl.BlockSpec((tm, 1), lambda i, j: (i, 0)),
            pl.BlockSpec((D, tn), lambda i, j: (0, clamp(j, 0, j_gate))),
            pl.BlockSpec((D, tn), lambda i, j: (0, clamp(j, j_gate, nj - j_gate))),
            pl.BlockSpec((1, tn), lambda i, j: (0, clamp(j, j_gate, nj - j_gate))),
        ],
        out_specs=[
            pl.BlockSpec((tm, tn), lambda i, j: (i, clamp(j, 0, j_gr))),
            pl.BlockSpec((tm, tn), lambda i, j: (i, clamp(j, j_gr, j_q - j_gr))),
            pl.BlockSpec((tm, tn), lambda i, j: (i, clamp(j, j_q, j_gate - j_q))),
            pl.BlockSpec((tm, tn), lambda i, j: (i, clamp(j, j_gate, nj - j_gate))),
        ],
        out_shape=[
            jax.ShapeDtypeStruct((T, rg_width), F32),
            jax.ShapeDtypeStruct((T, rg_width), BF16),
            jax.ShapeDtypeStruct((T, 3 * sb_width), BF16),
            jax.ShapeDtypeStruct((T, n_gate), BF16),
        ],
        compiler_params=_params(("arbitrary", "arbitrary")),
        name="mix_in",
    )(xg, r, w_in, w_bg, b_gate)


def _shift_rows(x, s, fill):
    n = x.shape[0]
    if s % SUBLANES == 0:
        head = jnp.full((s, x.shape[1]), fill, x.dtype)
        return jnp.concatenate([head, x[:n - s]], axis=0)
    rolled = pltpu.roll(x, s, axis=0)
    rows = lax.broadcasted_iota(jnp.int32, (SUBLANES, x.shape[1]), 0)
    head = jnp.where(rows < s, fill, rolled[:SUBLANES])
    return jnp.concatenate([head, rolled[SUBLANES:]], axis=0)


def _rglru_kernel(xr_ref, gr_ref, wc_ref, bc_ref, wax_ref, ba_ref, bx_ref, lam_ref, y_ref,
                  tail_ref, hc_ref, *, n_conv, bd):
    t = pl.program_id(2)

    @pl.when(t == 0)
    def _():
        tail_ref[...] = jnp.zeros_like(tail_ref)
        hc_ref[...] = jnp.zeros_like(hc_ref)

    x = xr_ref[...]
    tc, cw = x.shape
    tail = tail_ref[...]
    wc = wc_ref[...]
    xc = wc[n_conv - 1:n_conv] * x + bc_ref[...]
    for k in range(1, n_conv):
        prev = pltpu.roll(tail, k, axis=0)
        xc = xc + wc[n_conv - 1 - k:n_conv - k] * _shift_rows(x, k, prev)
    tail_ref[...] = x[tc - SUBLANES:]

    xb = xc.astype(BF16)
    r_parts, i_parts = [], []
    for gblk in range(cw // bd):
        ga = _dot(xb[:, gblk * bd:(gblk + 1) * bd], wax_ref[gblk])
        r_parts.append(ga[:, :bd])
        i_parts.append(ga[:, bd:])
    r = jax.nn.sigmoid(jnp.concatenate(r_parts, axis=1) + ba_ref[...])
    ig = jax.nn.sigmoid(jnp.concatenate(i_parts, axis=1) + bx_ref[...])

    nl = -lam_ref[...]
    softplus_nl = jnp.maximum(nl, 0.0) + jnp.log(1.0 + jnp.exp(-jnp.abs(nl)))
    log_a = (-RG_C) * r * softplus_nl
    a = jnp.exp(log_a)
    mult = jnp.sqrt(jnp.tanh(-log_a) * (a * a + 1.0))
    u = xc * ig * mult

    ca, cb = a, u
    s = 1
    while s < tc:
        cb = ca * _shift_rows(cb, s, 0.0) + cb
        ca = ca * _shift_rows(ca, s, 1.0)
        s *= 2
    h = ca * hc_ref[...] + cb
    hc_ref[...] = h[tc - 1:tc]
    y_ref[...] = (h * gr_ref[...].astype(F32)).astype(BF16)


def _rglru(xr, gr, w_conv, b_conv, wax, b_a, b_x, lam, *, tc, cw):
    B, S, C = xr.shape
    n_conv = w_conv.shape[0]
    bd = wax.shape[1]
    row = lambda b, c, t: (0, c)
    act = pl.BlockSpec((None, tc, cw), lambda b, c, t: (b, t, c))
    kern = functools.partial(_rglru_kernel, n_conv=n_conv, bd=bd)
    return pl.pallas_call(
        kern,
        grid=(B, C // cw, S // tc),
        in_specs=[
            act, act,
            pl.BlockSpec((n_conv, cw), row),
            pl.BlockSpec((1, cw), row),
            pl.BlockSpec((cw // bd, bd, 2 * bd), lambda b, c, t: (c, 0, 0)),
            pl.BlockSpec((1, cw), row),
            pl.BlockSpec((1, cw), row),
            pl.BlockSpec((1, cw), row),
        ],
        out_specs=act,
        out_shape=jax.ShapeDtypeStruct((B, S, C), BF16),
        scratch_shapes=[pltpu.VMEM((SUBLANES, cw), F32), pltpu.VMEM((1, cw), F32)],
        compiler_params=_params(("arbitrary", "arbitrary", "arbitrary")),
        name="rglru",
    )(xr, gr, w_conv, b_conv, wax, b_a, b_x, lam)


def _sb_scores(z, mask):
    neg_abs = lax.bitcast_convert_type(lax.bitcast_convert_type(z, jnp.uint32) | jnp.uint32(0x80000000), F32)
    softplus = jnp.maximum(z, 0.0) + jnp.log(1.0 + jnp.exp2(neg_abs)) * LOG2E
    log_beta = z - softplus
    if mask is not None:
        softplus = jnp.where(mask, softplus, 0.0)
    hi = softplus.astype(BF16)
    lo = (softplus - hi.astype(F32)).astype(BF16)
    return log_beta, jnp.concatenate([hi, lo], axis=1), jnp.sum(softplus, axis=1, keepdims=True)


def _sb_weights(log_beta, suffix, later, mask):
    att = jnp.exp2(log_beta - suffix - later)
    if mask is not None:
        att = jnp.where(mask, att, 0.0)
    return att.astype(BF16)


def _sb_attn_kernel(q_ref, k_ref, v_ref, o_ref, *, dh):
    i = pl.program_id(2)
    tq = q_ref.shape[0]
    tk = tq
    n_heads = q_ref.shape[1] // dh
    rows = lax.broadcasted_iota(jnp.int32, (tk, tk), 0)
    cols = lax.broadcasted_iota(jnp.int32, (tk, tk), 1)
    strict = (rows > cols).astype(BF16)
    strict2 = jnp.concatenate([strict, strict], axis=0)
    causal = cols < rows

    def sweep(s0, carry, mask):
        heads = range(n_heads)
        lanes = [slice(h * dh, (h + 1) * dh) for h in heads]
        zs = [lax.dot_general(q_ref[:, lanes[h]], k_ref[pl.ds(s0, tk), lanes[h]],
                              (((1,), (1,)), ((), ())), preferred_element_type=F32) for h in heads]
        scores = [_sb_scores(zs[h], mask) for h in heads]
        suffixes = [_dot(scores[h][1], strict2) for h in heads]
        atts = [_sb_weights(scores[h][0], suffixes[h], carry[h][1], mask) for h in heads]
        pvs = [_dot(atts[h], v_ref[pl.ds(s0, tk), lanes[h]]) for h in heads]
        return tuple((carry[h][0] + pvs[h], carry[h][1] + scores[h][2]) for h in heads)

    carry = tuple((jnp.zeros((tq, dh), F32), jnp.zeros((tq, 1), F32)) for _ in range(n_heads))
    carry = sweep(pl.multiple_of(i * tq, tq), carry, causal)
    carry = lax.fori_loop(0, i, lambda n, c: sweep(pl.multiple_of((i - 1 - n) * tk, tk), c, None), carry)
    for h in range(n_heads):
        o_ref[:, h * dh:(h + 1) * dh] = carry[h][0].astype(BF16)


def _sb_attn(qkv, *, n_heads, head_dim, tq, heads_per_step):
    B, S, _ = qkv.shape
    hw = heads_per_step * head_dim
    n_groups = n_heads // heads_per_step
    kern = functools.partial(_sb_attn_kernel, dh=head_dim)
    return pl.pallas_call(
        kern,
        grid=(B, n_groups, S // tq),
        in_specs=[
            pl.BlockSpec((None, tq, hw), lambda b, h, i: (b, i, h)),
            pl.BlockSpec((None, S, hw), lambda b, h, i: (b, 0, n_groups + h)),
            pl.BlockSpec((None, S, hw), lambda b, h, i: (b, 0, 2 * n_groups + h)),
        ],
        out_specs=pl.BlockSpec((None, tq, hw), lambda b, h, i: (b, i, h)),
        out_shape=jax.ShapeDtypeStruct((B, S, n_heads * head_dim), BF16),
        compiler_params=_params(("arbitrary", "arbitrary", "arbitrary")),
        name="sb_attn",
    )(qkv, qkv, qkv)


def _col_tiles(n):
    step = MXU_COLS if n % MXU_COLS == 0 else n
    return [slice(c, c + step) for c in range(0, n, step)]


def _merge_kernel(yr_ref, ya_ref, w0_ref, w1_ref, g0_ref, g1_ref, o_ref):
    for cols in _col_tiles(o_ref.shape[1]):
        d0 = _dot(yr_ref[...], w0_ref[:, cols])
        d1 = _dot(ya_ref[...], w1_ref[:, cols])
        o_ref[:, cols] = (g0_ref[:, cols].astype(F32) * d0 + g1_ref[:, cols].astype(F32) * d1).astype(BF16)


def _merge(y_rec, y_att, w_branch, gates, *, tm, tn):
    T, C = y_rec.shape
    D = w_branch.shape[1]
    nj = D // tn
    return pl.pallas_call(
        _merge_kernel,
        grid=(T // tm, nj),
        in_specs=[
            pl.BlockSpec((tm, C), lambda i, j: (i, 0)),
            pl.BlockSpec((tm, C), lambda i, j: (i, 0)),
            pl.BlockSpec((C, tn), lambda i, j: (0, j)),
            pl.BlockSpec((C, tn), lambda i, j: (1, j)),
            pl.BlockSpec((tm, tn), lambda i, j: (i, j)),
            pl.BlockSpec((tm, tn), lambda i, j: (i, j + nj)),
        ],
        out_specs=pl.BlockSpec((tm, tn), lambda i, j: (i, j)),
        out_shape=jax.ShapeDtypeStruct((T, D), BF16),
        compiler_params=_params(("arbitrary", "arbitrary")),
        name="merge",
    )(y_rec, y_att, w_branch, w_branch, gates, gates)


def _emit_residual(x_new, cols, g_ref, o_ref, xg_ref):
    o_ref[:, cols] = x_new
    xg_ref[:, cols] = (x_new * g_ref[:, cols]).astype(BF16)
    return _sum_sq(x_new)


def _accumulate_r(ssq, ssq_ref, r_ref, j, nj, width):
    @pl.when(j == 0)
    def _():
        ssq_ref[...] = ssq

    @pl.when(j > 0)
    def _():
        ssq_ref[...] += ssq

    @pl.when(j == nj - 1)
    def _():
        r_ref[...] = lax.rsqrt(ssq_ref[...] * (1.0 / width) + EPS)


def _matmul_res_kernel(a_ref, w_ref, x_ref, g_ref, o_ref, xg_ref, r_ref, acc_ref, ssq_ref):
    j, k = pl.program_id(1), pl.program_id(2)
    nj, nk = pl.num_programs(1), pl.num_programs(2)
    tn = o_ref.shape[1]

    @pl.when(k == 0)
    def _():
        for cols in _col_tiles(tn):
            acc_ref[:, cols] = x_ref[:, cols] + _dot(a_ref[...], w_ref[:, cols])

    @pl.when((k > 0) & (k < nk - 1))
    def _():
        for cols in _col_tiles(tn):
            acc_ref[:, cols] += _dot(a_ref[...], w_ref[:, cols])

    @pl.when(k == nk - 1)
    def _():
        ssq = 0.0
        for cols in _col_tiles(tn):
            ssq = ssq + _emit_residual(acc_ref[:, cols] + _dot(a_ref[...], w_ref[:, cols]), cols,
                                       g_ref, o_ref, xg_ref)
        _accumulate_r(ssq, ssq_ref, r_ref, j, nj, nj * tn)


def _matmul_res_single_kernel(a_ref, w_ref, x_ref, g_ref, o_ref, xg_ref, r_ref, ssq_ref):
    j, nj = pl.program_id(1), pl.num_programs(1)
    tn = o_ref.shape[1]
    ssq = 0.0
    for cols in _col_tiles(tn):
        ssq = ssq + _emit_residual(x_ref[:, cols] + _dot(a_ref[...], w_ref[:, cols]), cols, g_ref, o_ref, xg_ref)
    _accumulate_r(ssq, ssq_ref, r_ref, j, nj, nj * tn)


def _matmul_res(a, w, x, g_next, *, tm, tn, tk):
    T, K = a.shape
    N = w.shape[1]
    out_shape = [jax.ShapeDtypeStruct((T, N), F32), jax.ShapeDtypeStruct((T, N), BF16),
                 jax.ShapeDtypeStruct((T, 1), F32)]
    if tk == K:
        tile = pl.BlockSpec((tm, tn), lambda i, j: (i, j))
        return pl.pallas_call(
            _matmul_res_single_kernel,
            grid=(T // tm, N // tn),
            in_specs=[
                pl.BlockSpec((tm, K), lambda i, j: (i, 0)),
                pl.BlockSpec((K, tn), lambda i, j: (0, j)),
                tile,
                pl.BlockSpec((1, tn), lambda i, j: (0, j)),
            ],
            out_specs=[tile, tile, pl.BlockSpec((tm, 1), lambda i, j: (i, 0))],
            out_shape=out_shape,
            scratch_shapes=[pltpu.VMEM((tm, 1), F32)],
            compiler_params=_params(("arbitrary", "arbitrary")),
            name="matmul_res",
        )(a, w, x, g_next)
    assert K // tk >= 2
    tile = pl.BlockSpec((tm, tn), lambda i, j, k: (i, j))
    return pl.pallas_call(
        _matmul_res_kernel,
        grid=(T // tm, N // tn, K // tk),
        in_specs=[
            pl.BlockSpec((tm, tk), lambda i, j, k: (i, k)),
            pl.BlockSpec((tk, tn), lambda i, j, k: (k, j)),
            tile,
            pl.BlockSpec((1, tn), lambda i, j, k: (0, j)),
        ],
        out_specs=[tile, tile, pl.BlockSpec((tm, 1), lambda i, j, k: (i, 0))],
        out_shape=out_shape,
        scratch_shapes=[pltpu.VMEM((tm, tn), F32), pltpu.VMEM((tm, 1), F32)],
        compiler_params=_params(("arbitrary", "arbitrary", "arbitrary")),
        name="matmul_res_k",
    )(a, w, x, g_next)


def _ffn_up_kernel(xg_ref, r_ref, wa_ref, wv_ref, ca_ref, cv_ref, ba_ref, bv_ref, o_ref, carry_ref,
                   *, n_conv, blocks_per_seq):
    i = pl.program_id(0)
    j = pl.program_id(1)
    tm, tn = o_ref.shape

    @pl.when(i == 0)
    def _():
        carry_ref[j] = jnp.zeros(carry_ref.shape[1:], F32)

    keep = (i % blocks_per_seq != 0).astype(F32)
    r = r_ref[...]

    def conv(half, w_ref, c_ref, b_ref, cols):
        up = _dot(xg_ref[...], w_ref[:, cols]) * r
        prev = carry_ref[j, half, :, cols] * keep
        carry_ref[j, half, :, cols] = up[tm - SUBLANES:]
        c = c_ref[:, cols]
        out = c[n_conv - 1:n_conv] * up + b_ref[:, cols]
        for k in range(1, n_conv):
            out = out + c[n_conv - 1 - k:n_conv - k] * _shift_rows(up, k, pltpu.roll(prev, k, axis=0))
        return out

    for cols in _col_tiles(tn):
        v_half = conv(1, wv_ref, cv_ref, bv_ref, cols)
        a_half = conv(0, wa_ref, ca_ref, ba_ref, cols)
        o_ref[:, cols] = (jax.nn.gelu(a_half) * v_half).astype(BF16)


def _ffn_up(xg, r, w_up, w_conv, b_conv, *, seq, tm, tn):
    T, D = xg.shape
    F = w_up.shape[1] // 2
    nj = F // tn
    n_conv = w_conv.shape[0]
    assert seq % tm == 0 and n_conv <= SUBLANES
    kern = functools.partial(_ffn_up_kernel, n_conv=n_conv, blocks_per_seq=seq // tm)
    lo = lambda i, j: (0, j)
    hi = lambda i, j: (0, j + nj)
    return pl.pallas_call(
        kern,
        grid=(T // tm, nj),
        in_specs=[
            pl.BlockSpec((tm, D), lambda i, j: (i, 0)),
            pl.BlockSpec((tm, 1), lambda i, j: (i, 0)),
            pl.BlockSpec((D, tn), lo),
            pl.BlockSpec((D, tn), hi),
            pl.BlockSpec((n_conv, tn), lo),
            pl.BlockSpec((n_conv, tn), hi),
            pl.BlockSpec((1, tn), lo),
            pl.BlockSpec((1, tn), hi),
        ],
        out_specs=pl.BlockSpec((tm, tn), lambda i, j: (i, j)),
        out_shape=jax.ShapeDtypeStruct((T, F), BF16),
        scratch_shapes=[pltpu.VMEM((nj, 2, SUBLANES, tn), F32)],
        compiler_params=_params(("arbitrary", "arbitrary")),
        name="ffn_up",
    )(xg, r, w_up, w_up, w_conv, w_conv, b_conv, b_conv)


def _ple_kernel(x_ref, xg_ref, r_ref, p_ref, wp_ref, wg_ref, b_ref, gn_ref, o_ref, xgn_ref, rn_ref, ssq_ref):
    j, nj = pl.program_id(1), pl.num_programs(1)
    tn = o_ref.shape[1]
    pb = p_ref[...].astype(BF16)
    r = r_ref[...]
    ssq = 0.0
    for cols in _col_tiles(tn):
        e = _dot(pb, wp_ref[:, cols])
        gate = jax.nn.sigmoid(_dot(xg_ref[...], wg_ref[:, cols]) * r + b_ref[:, cols])
        ssq = ssq + _emit_residual(x_ref[:, cols] + gate * e, cols, gn_ref, o_ref, xgn_ref)
    _accumulate_r(ssq, ssq_ref, rn_ref, j, nj, nj * tn)


def _ple(x, xg, r, p, w_ple, w_gate, b_gate, g_next, *, tm, tn):
    T, D = x.shape
    P = p.shape[1]
    tile = pl.BlockSpec((tm, tn), lambda i, j: (i, j))
    vec = pl.BlockSpec((1, tn), lambda i, j: (0, j))
    col = pl.BlockSpec((tm, 1), lambda i, j: (i, 0))
    return pl.pallas_call(
        _ple_kernel,
        grid=(T // tm, D // tn),
        in_specs=[
            tile,
            pl.BlockSpec((tm, D), lambda i, j: (i, 0)),
            col,
            pl.BlockSpec((tm, P), lambda i, j: (i, 0)),
            pl.BlockSpec((P, tn), lambda i, j: (0, j)),
            pl.BlockSpec((D, tn), lambda i, j: (0, j)),
            vec,
            vec,
        ],
        out_specs=[tile, tile, col],
        out_shape=[jax.ShapeDtypeStruct((T, D), F32), jax.ShapeDtypeStruct((T, D), BF16),
                   jax.ShapeDtypeStruct((T, 1), F32)],
        scratch_shapes=[pltpu.VMEM((tm, 1), F32)],
        compiler_params=_params(("arbitrary", "arbitrary")),
        name="ple",
    )(x, xg, r, p, w_ple, w_gate, b_gate, g_next)


def _norm_kernel(x_ref, r_ref, g_ref, o_ref):
    o_ref[...] = x_ref[...] * r_ref[...] * g_ref[...]


def _final_norm(x, r, g, *, tm):
    T, D = x.shape
    return pl.pallas_call(
        _norm_kernel,
        grid=(T // tm,),
        in_specs=[pl.BlockSpec((tm, D), lambda i: (i, 0)), pl.BlockSpec((tm, 1), lambda i: (i, 0)),
                  pl.BlockSpec((1, D), lambda i: (0, 0))],
        out_specs=pl.BlockSpec((tm, D), lambda i: (i, 0)),
        out_shape=jax.ShapeDtypeStruct((T, D), F32),
        compiler_params=_params(("arbitrary",)),
        name="final_norm",
    )(x, r, g)


def _tile(n, want):
    t = min(n, want)
    while n % t:
        t //= 2
    return t


def kernel(x, p, g_mix, w_in, w_rg_conv, b_rg_conv, w_rg_a, b_rg_a, w_rg_x, b_rg_x, rg_lambda,
           w_branch, w_branch_gate, b_branch_gate, w_out, g_ffn, w_up, w_ffn_conv, b_ffn_conv,
           w_down, g_ple, w_ple, w_ple_gate, b_ple_gate, g_final):
    B, S, D = x.shape
    T = B * S
    depth = w_in.shape[0]
    rg_width = w_rg_conv.shape[2]
    sb_width = (w_in.shape[2] - 2 * rg_width) // 3
    head_dim = SB_HEAD_DIM
    n_heads = sb_width // head_dim
    d_ff = w_down.shape[1]

    tm = _tile(S, 1024)
    row = lambda a: a.reshape(1, -1)
    w_branch_rows = w_branch.reshape(depth, -1, D)

    xf = x.reshape(T, D)
    xg, r = _prep(xf, row(g_mix[0]), tm=_tile(T, 256))
    for l in range(depth):
        xr, gr, qkv, gates = _mix_in(
            xg, r, _cast_layer(w_in, l), _cast_layer(w_branch_gate, l), row(b_branch_gate[l]),
            rg_width=rg_width, sb_width=sb_width, head_dim=head_dim, tm=tm, tn=_tile(rg_width, 512))

        wax = jnp.concatenate([w_rg_a[l], w_rg_x[l]], axis=2).astype(BF16)
        y_rec = _rglru(
            xr.reshape(B, S, rg_width), gr.reshape(B, S, rg_width), w_rg_conv[l], row(b_rg_conv[l]),
            wax, row(b_rg_a[l]), row(b_rg_x[l]), row(rg_lambda[l]),
            tc=_tile(S, 256), cw=_tile(rg_width, 512))
        y_att = _sb_attn(qkv.reshape(B, S, 3 * sb_width), n_heads=n_heads, head_dim=head_dim,
                         tq=_tile(S, 256), heads_per_step=min(4, n_heads))

        merged = _merge(y_rec.reshape(T, rg_width), y_att.reshape(T, sb_width),
                        _cast_layer(w_branch_rows, l), gates, tm=tm, tn=_tile(D, 512))
        xf, xg, r = _matmul_res(merged, _cast_layer(w_out, l), xf, row(g_ffn[l]),
                                tm=tm, tn=_tile(D, 512), tk=D)

        act = _ffn_up(xg, r, _cast_layer(w_up, l), w_ffn_conv[l], row(b_ffn_conv[l]),
                      seq=S, tm=tm, tn=_tile(d_ff, 512))
        xf, xg, r = _matmul_res(act, _cast_layer(w_down, l), xf, row(g_ple[l]),
                                tm=tm, tn=_tile(D, 512), tk=_tile(d_ff, 4096))

        g_next = g_mix[l + 1] if l + 1 < depth else g_final
        xf, xg, r = _ple(xf, xg, r, p[l].reshape(T, -1), _cast_layer(w_ple, l), _cast_layer(w_ple_gate, l),
                         row(b_ple_gate[l]), row(g_next), tm=tm, tn=_tile(D, 512))

    return _final_norm(xf, r, row(g_final), tm=_tile(T, 256)).reshape(B, S, D)
```

```python
import functools

import jax
import jax.numpy as jnp
from jax import lax
from jax.experimental import pallas as pl
from jax.experimental.pallas import tpu as pltpu

F32 = jnp.float32
BF16 = jnp.bfloat16

EPS = 1e-6
RG_C = 8.0
LOG2E = 1.4426950408889634
SB_HEAD_DIM = 128
SUBLANES = 8
MXU_COLS = 256
VMEM_LIMIT = 56 * 1024 * 1024


def _params(sem):
    return pltpu.CompilerParams(dimension_semantics=sem, vmem_limit_bytes=VMEM_LIMIT)


def _dot(a, b):
    return jnp.dot(a, b, preferred_element_type=F32)


def _sum_sq(x):
    return jnp.sum(x * x, axis=1, keepdims=True)


def _cast_kernel(w_ref, o_ref):
    o_ref[...] = w_ref[...].astype(BF16)


def _cast_layer(w, layer, *, tr=512, tc=2048):
    _, R, C = w.shape
    tr, tc = _tile(R, tr), _tile(C, tc)
    return pl.pallas_call(
        _cast_kernel,
        grid=(R // tr, C // tc),
        in_specs=[pl.BlockSpec((None, tr, tc), lambda i, j: (layer, i, j))],
        out_specs=pl.BlockSpec((tr, tc), lambda i, j: (i, j)),
        out_shape=jax.ShapeDtypeStruct((R, C), BF16),
        compiler_params=_params(("arbitrary", "arbitrary")),
        name="cast_w",
    )(w)


def _prep_kernel(x_ref, g_ref, xg_ref, r_ref):
    x = x_ref[...]
    xg_ref[...] = (x * g_ref[...]).astype(BF16)
    r_ref[...] = lax.rsqrt(_sum_sq(x) * (1.0 / x.shape[1]) + EPS)


def _prep(x, g, *, tm):
    T, D = x.shape
    return pl.pallas_call(
        _prep_kernel,
        grid=(T // tm,),
        in_specs=[pl.BlockSpec((tm, D), lambda i: (i, 0)), pl.BlockSpec((1, D), lambda i: (0, 0))],
        out_specs=[pl.BlockSpec((tm, D), lambda i: (i, 0)), pl.BlockSpec((tm, 1), lambda i: (i, 0))],
        out_shape=[jax.ShapeDtypeStruct((T, D), BF16), jax.ShapeDtypeStruct((T, 1), F32)],
        compiler_params=_params(("arbitrary",)),
        name="prep",
    )(x, g)


def _mix_in_kernel(xg_ref, r_ref, win_ref, wbg_ref, b_ref, xr_ref, gr_ref, qkv_ref, gate_ref,
                   *, j_gr, j_q, j_k, j_gate, q_scale):
    j = pl.program_id(1)

    def segment(pred, w_ref, o_ref, epilogue):
        @pl.when(pred)
        def _():
            r = r_ref[...]
            for cols in _col_tiles(o_ref.shape[1]):
                o_ref[:, cols] = epilogue(_dot(xg_ref[...], w_ref[:, cols]) * r, cols).astype(o_ref.dtype)

    segment(j < j_gr, win_ref, xr_ref, lambda acc, cols: acc)
    segment((j >= j_gr) & (j < j_q), win_ref, gr_ref, lambda acc, cols: jax.nn.gelu(acc))
    segment((j >= j_q) & (j < j_k), win_ref, qkv_ref, lambda acc, cols: acc * q_scale)
    segment((j >= j_k) & (j < j_gate), win_ref, qkv_ref, lambda acc, cols: acc)
    segment(j >= j_gate, wbg_ref, gate_ref, lambda acc, cols: jax.nn.sigmoid(acc + b_ref[:, cols]))


def _mix_in(xg, r, w_in, w_bg, b_gate, *, rg_width, sb_width, head_dim, tm, tn):
    T, D = xg.shape
    n_in, n_gate = w_in.shape[1], w_bg.shape[1]
    j_gr = rg_width // tn
    j_q = 2 * rg_width // tn
    j_k = j_q + sb_width // tn
    j_gate = n_in // tn
    nj = j_gate + n_gate // tn
    assert rg_width % tn == 0 and sb_width % tn == 0 and n_in == 2 * rg_width + 3 * sb_width

    def clamp(j, lo, n):
        return jnp.clip(j - lo, 0, n - 1)

    kern = functools.partial(_mix_in_kernel, j_gr=j_gr, j_q=j_q, j_k=j_k, j_gate=j_gate,
                             q_scale=LOG2E * head_dim ** -0.5)
    return pl.pallas_call(
        kern,
        grid=(T // tm, nj),
        in_specs=[
            pl.BlockSpec((tm, D), lambda i, j: (i, 0)),
            pl.BlockSpec((tm, 1), lambda i, j: (i, 0)),
            pl.BlockSpec((D, tn), lambda i, j: (0, clamp(j, 0, j_gate))),
            pl.BlockSpec((D, tn), lambda i, j: (0, clamp(j, j_gate, nj - j_gate))),
            pl.BlockSpec((1, tn), lambda i, j: (0, clamp(j, j_gate, nj - j_gate))),
        ],
        out_specs=[
            pl.BlockSpec((tm, tn), lambda i, j: (i, clamp(j, 0, j_gr))),
            pl.BlockSpec((tm, tn), lambda i, j: (i, clamp(j, j_gr, j_q - j_gr))),
            pl.BlockSpec((tm, tn), lambda i, j: (i, clamp(j, j_q, j_gate - j_q))),
            pl.BlockSpec((tm, tn), lambda i, j: (i, clamp(j, j_gate, nj - j_gate))),
        ],
        out_shape=[
            jax.ShapeDtypeStruct((T, rg_width), F32),
            jax.ShapeDtypeStruct((T, rg_width), BF16),
            jax.ShapeDtypeStruct((T, 3 * sb_width), BF16),
            jax.ShapeDtypeStruct((T, n_gate), BF16),
        ],
        compiler_params=_params(("arbitrary", "arbitrary")),
        name="mix_in",
    )(xg, r, w_in, w_bg, b_gate)


def _shift_rows(x, s, fill):
    n = x.shape[0]
    if s % SUBLANES == 0:
        head = jnp.full((s, x.shape[1]), fill, x.dtype)
        return jnp.concatenate([head, x[:n - s]], axis=0)
    rolled = pltpu.roll(x, s, axis=0)
    rows = lax.broadcasted_iota(jnp.int32, (SUBLANES, x.shape[1]), 0)
    head = jnp.where(rows < s, fill, rolled[:SUBLANES])
    return jnp.concatenate([head, rolled[SUBLANES:]], axis=0)


def _rglru_kernel(xr_ref, gr_ref, wc_ref, bc_ref, wax_ref, ba_ref, bx_ref, lam_ref, y_ref,
                  tail_ref, hc_ref, *, n_conv, bd):
    t = pl.program_id(2)

    @pl.when(t == 0)
    def _():
        tail_ref[...] = jnp.zeros_like(tail_ref)
        hc_ref[...] = jnp.zeros_like(hc_ref)

    x = xr_ref[...]
    tc, cw = x.shape
    tail = tail_ref[...]
    wc = wc_ref[...]
    xc = wc[n_conv - 1:n_conv] * x + bc_ref[...]
    for k in range(1, n_conv):
        prev = pltpu.roll(tail, k, axis=0)
        xc = xc + wc[n_conv - 1 - k:n_conv - k] * _shift_rows(x, k, prev)
    tail_ref[...] = x[tc - SUBLANES:]

    xb = xc.astype(BF16)
    r_parts, i_parts = [], []
    for gblk in range(cw // bd):
        ga = _dot(xb[:, gblk * bd:(gblk + 1) * bd], wax_ref[gblk])
        r_parts.append(ga[:, :bd])
        i_parts.append(ga[:, bd:])
    r = jax.nn.sigmoid(jnp.concatenate(r_parts, axis=1) + ba_ref[...])
    ig = jax.nn.sigmoid(jnp.concatenate(i_parts, axis=1) + bx_ref[...])

    nl = -lam_ref[...]
    softplus_nl = jnp.maximum(nl, 0.0) + jnp.log(1.0 + jnp.exp(-jnp.abs(nl)))
    log_a = (-RG_C) * r * softplus_nl
    a = jnp.exp(log_a)
    mult = jnp.sqrt(jnp.tanh(-log_a) * (a * a + 1.0))
    u = xc * ig * mult

    ca, cb = a, u
    s = 1
    while s < tc:
        cb = ca * _shift_rows(cb, s, 0.0) + cb
        ca = ca * _shift_rows(ca, s, 1.0)
        s *= 2
    h = ca * hc_ref[...] + cb
    hc_ref[...] = h[tc - 1:tc]
    y_ref[...] = (h * gr_ref[...].astype(F32)).astype(BF16)


def _rglru(xr, gr, w_conv, b_conv, wax, b_a, b_x, lam, *, tc, cw):
    B, S, C = xr.shape
    n_conv = w_conv.shape[0]
    bd = wax.shape[1]
    row = lambda b, c, t: (0, c)
    act = pl.BlockSpec((None, tc, cw), lambda b, c, t: (b, t, c))
    kern = functools.partial(_rglru_kernel, n_conv=n_conv, bd=bd)
    return pl.pallas_call(
        kern,
        grid=(B, C // cw, S // tc),
        in_specs=[
            act, act,
            pl.BlockSpec((n_conv, cw), row),
            pl.BlockSpec((1, cw), row),
            pl.BlockSpec((cw // bd, bd, 2 * bd), lambda b, c, t: (c, 0, 0)),
            pl.BlockSpec((1, cw), row),
            pl.BlockSpec((1, cw), row),
            pl.BlockSpec((1, cw), row),
        ],
        out_specs=act,
        out_shape=jax.ShapeDtypeStruct((B, S, C), BF16),
        scratch_shapes=[pltpu.VMEM((SUBLANES, cw), F32), pltpu.VMEM((1, cw), F32)],
        compiler_params=_params(("arbitrary", "arbitrary", "arbitrary")),
        name="rglru",
    )(xr, gr, w_conv, b_conv, wax, b_a, b_x, lam)


def _sb_scores(z, mask):
    neg_abs = lax.bitcast_convert_type(lax.bitcast_convert_type(z, jnp.uint32) | jnp.uint32(0x80000000), F32)
    softplus = jnp.maximum(z, 0.0) + jnp.log(1.0 + jnp.exp2(neg_abs)) * LOG2E
    log_beta = z - softplus
    if mask is not None:
        softplus = jnp.where(mask, softplus, 0.0)
    hi = softplus.astype(BF16)
    lo = (softplus - hi.astype(F32)).astype(BF16)
    return log_beta, jnp.concatenate([hi, lo], axis=1), jnp.sum(softplus, axis=1, keepdims=True)


def _sb_weights(log_beta, suffix, later, mask):
    att = jnp.exp2(log_beta - suffix - later)
    if mask is not None:
        att = jnp.where(mask, att, 0.0)
    return att.astype(BF16)


def _sb_attn_kernel(q_ref, k_ref, v_ref, o_ref, *, dh):
    i = pl.program_id(2)
    tq = q_ref.shape[0]
    tk = tq
    n_heads = q_ref.shape[1] // dh
    rows = lax.broadcasted_iota(jnp.int32, (tk, tk), 0)
    cols = lax.broadcasted_iota(jnp.int32, (tk, tk), 1)
    strict = (rows > cols).astype(BF16)
    strict2 = jnp.concatenate([strict, strict], axis=0)
    causal = cols < rows

    def sweep(s0, carry, mask):
        heads = range(n_heads)
        lanes = [slice(h * dh, (h + 1) * dh) for h in heads]
        zs = [lax.dot_general(q_ref[:, lanes[h]], k_ref[pl.ds(s0, tk), lanes[h]],
                              (((1,), (1,)), ((), ())), preferred_element_type=F32) for h in heads]
        scores = [_sb_scores(zs[h], mask) for h in heads]
        suffixes = [_dot(scores[h][1], strict2) for h in heads]
        atts = [_sb_weights(scores[h][0], suffixes[h], carry[h][1], mask) for h in heads]
        pvs = [_dot(atts[h], v_ref[pl.ds(s0, tk), lanes[h]]) for h in heads]
        return tuple((carry[h][0] + pvs[h], carry[h][1] + scores[h][2]) for h in heads)

    carry = tuple((jnp.zeros((tq, dh), F32), jnp.zeros((tq, 1), F32)) for _ in range(n_heads))
    carry = sweep(pl.multiple_of(i * tq, tq), carry, causal)
    carry = lax.fori_loop(0, i, lambda n, c: sweep(pl.multiple_of((i - 1 - n) * tk, tk), c, None), carry)
    for h in range(n_heads):
        o_ref[:, h * dh:(h + 1) * dh] = carry[h][0].astype(BF16)


def _sb_attn(qkv, *, n_heads, head_dim, tq, heads_per_step):
    B, S, _ = qkv.shape
    hw = heads_per_step * head_dim
    n_groups = n_heads // heads_per_step
    kern = functools.partial(_sb_attn_kernel, dh=head_dim)
    return pl.pallas_call(
        kern,
        grid=(B, n_groups, S // tq),
        in_specs=[
            pl.BlockSpec((None, tq, hw), lambda b, h, i: (b, i, h)),
            pl.BlockSpec((None, S, hw), lambda b, h, i: (b, 0, n_groups + h)),
            pl.BlockSpec((None, S, hw), lambda b, h, i: (b, 0, 2 * n_groups + h)),
        ],
        out_specs=pl.BlockSpec((None, tq, hw), lambda b, h, i: (b, i, h)),
        out_shape=jax.ShapeDtypeStruct((B, S, n_heads * head_dim), BF16),
        compiler_params=_params(("arbitrary", "arbitrary", "arbitrary")),
        name="sb_attn",
    )(qkv, qkv, qkv)


def _col_tiles(n):
    step = MXU_COLS if n % MXU_COLS == 0 else n
    return [slice(c, c + step) for c in range(0, n, step)]


def _merge_kernel(yr_ref, ya_ref, w0_ref, w1_ref, g0_ref, g1_ref, o_ref):
    for cols in _col_tiles(o_ref.shape[1]):
        d0 = _dot(yr_ref[...], w0_ref[:, cols])
        d1 = _dot(ya_ref[...], w1_ref[:, cols])
        o_ref[:, cols] = (g0_ref[:, cols].astype(F32) * d0 + g1_ref[:, cols].astype(F32) * d1).astype(BF16)


def _merge(y_rec, y_att, w_branch, gates, *, tm, tn):
    T, C = y_rec.shape
    D = w_branch.shape[1]
    nj = D // tn
    return pl.pallas_call(
        _merge_kernel,
        grid=(T // tm, nj),
        in_specs=[
            pl.BlockSpec((tm, C), lambda i, j: (i, 0)),
            pl.BlockSpec((tm, C), lambda i, j: (i, 0)),
            pl.BlockSpec((C, tn), lambda i, j: (0, j)),
            pl.BlockSpec((C, tn), lambda i, j: (1, j)),
            pl.BlockSpec((tm, tn), lambda i, j: (i, j)),
            pl.BlockSpec((tm, tn), lambda i, j: (i, j + nj)),
        ],
        out_specs=pl.BlockSpec((tm, tn), lambda i, j: (i, j)),
        out_shape=jax.ShapeDtypeStruct((T, D), BF16),
        compiler_params=_params(("arbitrary", "arbitrary")),
        name="merge",
    )(y_rec, y_att, w_branch, w_branch, gates, gates)


def _emit_residual(x_new, cols, g_ref, o_ref, xg_ref):
    o_ref[:, cols] = x_new
    xg_ref[:, cols] = (x_new * g_ref[:, cols]).astype(BF16)
    return _sum_sq(x_new)


def _accumulate_r(ssq, ssq_ref, r_ref, j, nj, width):
    @pl.when(j == 0)
    def _():
        ssq_ref[...] = ssq

    @pl.when(j > 0)
    def _():
        ssq_ref[...] += ssq

    @pl.when(j == nj - 1)
    def _():
        r_ref[...] = lax.rsqrt(ssq_ref[...] * (1.0 / width) + EPS)


def _matmul_res_kernel(a_ref, w_ref, x_ref, g_ref, o_ref, xg_ref, r_ref, acc_ref, ssq_ref):
    j, k = pl.program_id(1), pl.program_id(2)
    nj, nk = pl.num_programs(1), pl.num_programs(2)
    tn = o_ref.shape[1]

    @pl.when(k == 0)
    def _():
        for cols in _col_tiles(tn):
            acc_ref[:, cols] = x_ref[:, cols] + _dot(a_ref[...], w_ref[:, cols])

    @pl.when((k > 0) & (k < nk - 1))
    def _():
        for cols in _col_tiles(tn):
            acc_ref[:, cols] += _dot(a_ref[...], w_ref[:, cols])

    @pl.when(k == nk - 1)
    def _():
        ssq = 0.0
        for cols in _col_tiles(tn):
            ssq = ssq + _emit_residual(acc_ref[:, cols] + _dot(a_ref[...], w_ref[:, cols]), cols,
                                       g_ref, o_ref, xg_ref)
        _accumulate_r(ssq, ssq_ref, r_ref, j, nj, nj * tn)


def _matmul_res_single_kernel(a_ref, w_ref, x_ref, g_ref, o_ref, xg_ref, r_ref, ssq_ref):
    j, nj = pl.program_id(1), pl.num_programs(1)
    tn = o_ref.shape[1]
    ssq = 0.0
    for cols in _col_tiles(tn):
        ssq = ssq + _emit_residual(x_ref[:, cols] + _dot(a_ref[...], w_ref[:, cols]), cols, g_ref, o_ref, xg_ref)
    _accumulate_r(ssq, ssq_ref, r_ref, j, nj, nj * tn)


def _matmul_res(a, w, x, g_next, *, tm, tn, tk):
    T, K = a.shape
    N = w.shape[1]
    out_shape = [jax.ShapeDtypeStruct((T, N), F32), jax.ShapeDtypeStruct((T, N), BF16),
                 jax.ShapeDtypeStruct((T, 1), F32)]
    if tk == K:
        tile = pl.BlockSpec((tm, tn), lambda i, j: (i, j))
        return pl.pallas_call(
            _matmul_res_single_kernel,
            grid=(T // tm, N // tn),
            in_specs=[
                pl.BlockSpec((tm, K), lambda i, j: (i, 0)),
                pl.BlockSpec((K, tn), lambda i, j: (0, j)),
                tile,
                pl.BlockSpec((1, tn), lambda i, j: (0, j)),
            ],
            out_specs=[tile, tile, pl.BlockSpec((tm, 1), lambda i, j: (i, 0))],
            out_shape=out_shape,
            scratch_shapes=[pltpu.VMEM((tm, 1), F32)],
            compiler_params=_params(("arbitrary", "arbitrary")),
            name="matmul_res",
        )(a, w, x, g_next)
    assert K // tk >= 2
    tile = pl.BlockSpec((tm, tn), lambda i, j, k: (i, j))
    return pl.pallas_call(
        _matmul_res_kernel,
        grid=(T // tm, N // tn, K // tk),
        in_specs=[
            pl.BlockSpec((tm, tk), lambda i, j, k: (i, k)),
            pl.BlockSpec((tk, tn), lambda i, j, k: (k, j)),
            tile,
            pl.BlockSpec((1, tn), lambda i, j, k: (0, j)),
        ],
        out_specs=[tile, tile, pl.BlockSpec((tm, 1), lambda i, j, k: (i, 0))],
        out_shape=out_shape,
        scratch_shapes=[pltpu.VMEM((tm, tn), F32), pltpu.VMEM((tm, 1), F32)],
        compiler_params=_params(("arbitrary", "arbitrary", "arbitrary")),
        name="matmul_res_k",
    )(a, w, x, g_next)


def _ffn_up_kernel(xg_ref, r_ref, wa_ref, wv_ref, ca_ref, cv_ref, ba_ref, bv_ref, o_ref, carry_ref, ubuf_ref,
                   *, n_conv, blocks_per_seq):
    i = pl.program_id(0)
    j = pl.program_id(1)
    tm, tn = o_ref.shape

    @pl.when(i == 0)
    def _():
        carry_ref[j] = jnp.zeros(carry_ref.shape[1:], F32)

    keep = (i % blocks_per_seq != 0).astype(F32)
    r = r_ref[...]

    def conv(half, w_ref, c_ref, b_ref, cols):
        up = _dot(xg_ref[...], w_ref[:, cols]) * r
        width = up.shape[1]
        ubuf_ref[half, 0:SUBLANES, 0:width] = carry_ref[j, half, :, cols] * keep
        ubuf_ref[half, SUBLANES:, 0:width] = up
        carry_ref[j, half, :, cols] = up[tm - SUBLANES:]
        c = c_ref[:, cols]
        out = c[n_conv - 1:n_conv] * up + b_ref[:, cols]
        for k in range(1, n_conv):
            out = out + c[n_conv - 1 - k:n_conv - k] * ubuf_ref[half, SUBLANES - k:SUBLANES - k + tm, 0:width]
        return out

    for cols in _col_tiles(tn):
        v_half = conv(1, wv_ref, cv_ref, bv_ref, cols)
        a_half = conv(0, wa_ref, ca_ref, ba_ref, cols)
        o_ref[:, cols] = (jax.nn.gelu(a_half) * v_half).astype(BF16)


def _ffn_up(xg, r, w_up, w_conv, b_conv, *, seq, tm, tn):
    T, D = xg.shape
    F = w_up.shape[1] // 2
    nj = F // tn
    n_conv = w_conv.shape[0]
    assert seq % tm == 0 and n_conv <= SUBLANES
    kern = functools.partial(_ffn_up_kernel, n_conv=n_conv, blocks_per_seq=seq // tm)
    lo = lambda i, j: (0, j)
    hi = lambda i, j: (0, j + nj)
    return pl.pallas_call(
        kern,
        grid=(T // tm, nj),
        in_specs=[
            pl.BlockSpec((tm, D), lambda i, j: (i, 0)),
            pl.BlockSpec((tm, 1), lambda i, j: (i, 0)),
            pl.BlockSpec((D, tn), lo),
            pl.BlockSpec((D, tn), hi),
            pl.BlockSpec((n_conv, tn), lo),
            pl.BlockSpec((n_conv, tn), hi),
            pl.BlockSpec((1, tn), lo),
            pl.BlockSpec((1, tn), hi),
        ],
        out_specs=pl.BlockSpec((tm, tn), lambda i, j: (i, j)),
        out_shape=jax.ShapeDtypeStruct((T, F), BF16),
        scratch_shapes=[pltpu.VMEM((nj, 2, SUBLANES, tn), F32),
                        pltpu.VMEM((2, tm + SUBLANES, min(tn, MXU_COLS)), F32)],
        compiler_params=_params(("arbitrary", "arbitrary")),
        name="ffn_up",
    )(xg, r, w_up, w_up, w_conv, w_conv, b_conv, b_conv)


def _ple_kernel(x_ref, xg_ref, r_ref, p_ref, wp_ref, wg_ref, b_ref, gn_ref, o_ref, xgn_ref, rn_ref, ssq_ref):
    j, nj = pl.program_id(1), pl.num_programs(1)
    tn = o_ref.shape[1]
    pb = p_ref[...].astype(BF16)
    r = r_ref[...]
    ssq = 0.0
    for cols in _col_tiles(tn):
        e = _dot(pb, wp_ref[:, cols])
        gate = jax.nn.sigmoid(_dot(xg_ref[...], wg_ref[:, cols]) * r + b_ref[:, cols])
        ssq = ssq + _emit_residual(x_ref[:, cols] + gate * e, cols, gn_ref, o_ref, xgn_ref)
    _accumulate_r(ssq, ssq_ref, rn_ref, j, nj, nj * tn)


def _ple(x, xg, r, p, w_ple, w_gate, b_gate, g_next, *, tm, tn):
    T, D = x.shape
    P = p.shape[1]
    tile = pl.BlockSpec((tm, tn), lambda i, j: (i, j))
    vec = pl.BlockSpec((1, tn), lambda i, j: (0, j))
    col = pl.BlockSpec((tm, 1), lambda i, j: (i, 0))
    return pl.pallas_call(
        _ple_kernel,
        grid=(T // tm, D // tn),
        in_specs=[
            tile,
            pl.BlockSpec((tm, D), lambda i, j: (i, 0)),
            col,
            pl.BlockSpec((tm, P), lambda i, j: (i, 0)),
            pl.BlockSpec((P, tn), lambda i, j: (0, j)),
            pl.BlockSpec((D, tn), lambda i, j: (0, j)),
            vec,
            vec,
        ],
        out_specs=[tile, tile, col],
        out_shape=[jax.ShapeDtypeStruct((T, D), F32), jax.ShapeDtypeStruct((T, D), BF16),
                   jax.ShapeDtypeStruct((T, 1), F32)],
        scratch_shapes=[pltpu.VMEM((tm, 1), F32)],
        compiler_params=_params(("arbitrary", "arbitrary")),
        name="ple",
    )(x, xg, r, p, w_ple, w_gate, b_gate, g_next)


def _norm_kernel(x_ref, r_ref, g_ref, o_ref):
    o_ref[...] = x_ref[...] * r_ref[...] * g_ref[...]


def _final_norm(x, r, g, *, tm):
    T, D = x.shape
    return pl.pallas_call(
        _norm_kernel,
        grid=(T // tm,),
        in_specs=[pl.BlockSpec((tm, D), lambda i: (i, 0)), pl.BlockSpec((tm, 1), lambda i: (i, 0)),
                  pl.BlockSpec((1, D), lambda i: (0, 0))],
        out_specs=pl.BlockSpec((tm, D), lambda i: (i, 0)),
        out_shape=jax.ShapeDtypeStruct((T, D), F32),
        compiler_params=_params(("arbitrary",)),
        name="final_norm",
    )(x, r, g)


def _tile(n, want):
    t = min(n, want)
    while n % t:
        t //= 2
    return t


def kernel(x, p, g_mix, w_in, w_rg_conv, b_rg_conv, w_rg_a, b_rg_a, w_rg_x, b_rg_x, rg_lambda,
           w_branch, w_branch_gate, b_branch_gate, w_out, g_ffn, w_up, w_ffn_conv, b_ffn_conv,
           w_down, g_ple, w_ple, w_ple_gate, b_ple_gate, g_final):
    B, S, D = x.shape
    T = B * S
    depth = w_in.shape[0]
    rg_width = w_rg_conv.shape[2]
    sb_width = (w_in.shape[2] - 2 * rg_width) // 3
    head_dim = SB_HEAD_DIM
    n_heads = sb_width // head_dim
    d_ff = w_down.shape[1]

    tm = _tile(S, 1024)
    row = lambda a: a.reshape(1, -1)
    w_branch_rows = w_branch.reshape(depth, -1, D)

    xf = x.reshape(T, D)
    xg, r = _prep(xf, row(g_mix[0]), tm=_tile(T, 256))
    for l in range(depth):
        xr, gr, qkv, gates = _mix_in(
            xg, r, _cast_layer(w_in, l), _cast_layer(w_branch_gate, l), row(b_branch_gate[l]),
            rg_width=rg_width, sb_width=sb_width, head_dim=head_dim, tm=tm, tn=_tile(rg_width, 512))

        wax = jnp.concatenate([w_rg_a[l], w_rg_x[l]], axis=2).astype(BF16)
        y_rec = _rglru(
            xr.reshape(B, S, rg_width), gr.reshape(B, S, rg_width), w_rg_conv[l], row(b_rg_conv[l]),
            wax, row(b_rg_a[l]), row(b_rg_x[l]), row(rg_lambda[l]),
            tc=_tile(S, 256), cw=_tile(rg_width, 512))
        y_att = _sb_attn(qkv.reshape(B, S, 3 * sb_width), n_heads=n_heads, head_dim=head_dim,
                         tq=_tile(S, 256), heads_per_step=min(8, n_heads))

        merged = _merge(y_rec.reshape(T, rg_width), y_att.reshape(T, sb_width),
                        _cast_layer(w_branch_rows, l), gates, tm=tm, tn=_tile(D, 512))
        xf, xg, r = _matmul_res(merged, _cast_layer(w_out, l), xf, row(g_ffn[l]),
                                tm=tm, tn=_tile(D, 512), tk=D)

        act = _ffn_up(xg, r, _cast_layer(w_up, l), w_ffn_conv[l], row(b_ffn_conv[l]),
                      seq=S, tm=tm, tn=_tile(d_ff, 512))
        xf, xg, r = _matmul_res(act, _cast_layer(w_down, l), xf, row(g_ple[l]),
                                tm=tm, tn=_tile(D, 1024), tk=_tile(d_ff, 2048))

        g_next = g_mix[l + 1] if l + 1 < depth else g_final
        xf, xg, r = _ple(xf, xg, r, p[l].reshape(T, -1), _cast_layer(w_ple, l), _cast_layer(w_ple_gate, l),
                         row(b_ple_gate[l]), row(g_next), tm=tm, tn=_tile(D, 512))

    return _final_norm(xf, r, row(g_final), tm=_tile(T, 256)).reshape(B, S, D)
```

```python
import functools

import jax
import jax.numpy as jnp
from jax import lax
from jax.experimental import pallas as pl
from jax.experimental.pallas import tpu as pltpu

F32 = jnp.float32
BF16 = jnp.bfloat16

EPS = 1e-6
RG_C = 8.0
LOG2E = 1.4426950408889634
SB_HEAD_DIM = 128
SUBLANES = 8
MXU_COLS = 256
VMEM_LIMIT = 56 * 1024 * 1024


def _params(sem):
    return pltpu.CompilerParams(dimension_semantics=sem, vmem_limit_bytes=VMEM_LIMIT)


def _dot(a, b):
    return jnp.dot(a, b, preferred_element_type=F32)


def _sum_sq(x):
    return jnp.sum(x * x, axis=1, keepdims=True)


def _cast_kernel(w_ref, o_ref):
    o_ref[...] = w_ref[...].astype(BF16)


def _cast_layer(w, layer, *, tr=512, tc=2048):
    _, R, C = w.shape
    tr, tc = _tile(R, tr), _tile(C, tc)
    return pl.pallas_call(
        _cast_kernel,
        grid=(R // tr, C // tc),
        in_specs=[pl.BlockSpec((None, tr, tc), lambda i, j: (layer, i, j))],
        out_specs=pl.BlockSpec((tr, tc), lambda i, j: (i, j)),
        out_shape=jax.ShapeDtypeStruct((R, C), BF16),
        compiler_params=_params(("arbitrary", "arbitrary")),
        name="cast_w",
    )(w)


def _prep_kernel(x_ref, g_ref, xg_ref, r_ref):
    x = x_ref[...]
    xg_ref[...] = (x * g_ref[...]).astype(BF16)
    r_ref[...] = lax.rsqrt(_sum_sq(x) * (1.0 / x.shape[1]) + EPS)


def _prep(x, g, *, tm):
    T, D = x.shape
    return pl.pallas_call(
        _prep_kernel,
        grid=(T // tm,),
        in_specs=[pl.BlockSpec((tm, D), lambda i: (i, 0)), pl.BlockSpec((1, D), lambda i: (0, 0))],
        out_specs=[pl.BlockSpec((tm, D), lambda i: (i, 0)), pl.BlockSpec((tm, 1), lambda i: (i, 0))],
        out_shape=[jax.ShapeDtypeStruct((T, D), BF16), jax.ShapeDtypeStruct((T, 1), F32)],
        compiler_params=_params(("arbitrary",)),
        name="prep",
    )(x, g)


def _mix_in_kernel(xg_ref, r_ref, win_ref, wbg_ref, b_ref, xr_ref, gr_ref, qkv_ref, gate_ref,
                   *, j_gr, j_q, j_k, j_gate, q_scale):
    j = pl.program_id(1)

    def segment(pred, w_ref, o_ref, epilogue):
        @pl.when(pred)
        def _():
            r = r_ref[...]
            for cols in _col_tiles(o_ref.shape[1]):
                o_ref[:, cols] = epilogue(_dot(xg_ref[...], w_ref[:, cols]) * r, cols).astype(o_ref.dtype)

    segment(j < j_gr, win_ref, xr_ref, lambda acc, cols: acc)
    segment((j >= j_gr) & (j < j_q), win_ref, gr_ref, lambda acc, cols: jax.nn.gelu(acc))
    segment((j >= j_q) & (j < j_k), win_ref, qkv_ref, lambda acc, cols: acc * q_scale)
    segment((j >= j_k) & (j < j_gate), win_ref, qkv_ref, lambda acc, cols: acc)
    segment(j >= j_gate, wbg_ref, gate_ref, lambda acc, cols: jax.nn.sigmoid(acc + b_ref[:, cols]))


def _mix_in(xg, r, w_in, w_bg, b_gate, *, rg_width, sb_width, head_dim, tm, tn):
    T, D = xg.shape
    n_in, n_gate = w_in.shape[1], w_bg.shape[1]
    j_gr = rg_width // tn
    j_q = 2 * rg_width // tn
    j_k = j_q + sb_width // tn
    j_gate = n_in // tn
    nj = j_gate + n_gate // tn
    assert rg_width % tn == 0 and sb_width % tn == 0 and n_in == 2 * rg_width + 3 * sb_width

    def clamp(j, lo, n):
        return jnp.clip(j - lo, 0, n - 1)

    kern = functools.partial(_mix_in_kernel, j_gr=j_gr, j_q=j_q, j_k=j_k, j_gate=j_gate,
                             q_scale=LOG2E * head_dim ** -0.5)
    return pl.pallas_call(
        kern,
        grid=(T // tm, nj),
        in_specs=[
            pl.BlockSpec((tm, D), lambda i, j: (i, 0)),
            pl.BlockSpec((tm, 1), lambda i, j: (i, 0)),
            pl.BlockSpec((D, tn), lambda i, j: (0, clamp(j, 0, j_gate))),
            pl.BlockSpec((D, tn), lambda i, j: (0, clamp(j, j_gate, nj - j_gate))),
            pl.BlockSpec((1, tn), lambda i, j: (0, clamp(j, j_gate, nj - j_gate))),
        ],
        out_specs=[
            pl.BlockSpec((tm, tn), lambda i, j: (i, clamp(j, 0, j_gr))),
            pl.BlockSpec((tm, tn), lambda i, j: (i, clamp(j, j_gr, j_q - j_gr))),
            pl.BlockSpec((tm, tn), lambda i, j: (i, clamp(j, j_q, j_gate - j_q))),
            pl.BlockSpec((tm, tn), lambda i, j: (i, clamp(j, j_gate, nj - j_gate))),
        ],
        out_shape=[
            jax.ShapeDtypeStruct((T, rg_width), F32),
            jax.ShapeDtypeStruct((T, rg_width), BF16),
            jax.ShapeDtypeStruct((T, 3 * sb_width), BF16),
            jax.ShapeDtypeStruct((T, n_gate), BF16),
        ],
        compiler_params=_params(("arbitrary", "arbitrary")),
        name="mix_in",
    )(xg, r, w_in, w_bg, b_gate)


def _shift_rows(x, s, fill):
    n = x.shape[0]
    if s % SUBLANES == 0:
        head = jnp.full((s, x.shape[1]), fill, x.dtype)
        return jnp.concatenate([head, x[:n - s]], axis=0)
    rolled = pltpu.roll(x, s, axis=0)
    rows = lax.broadcasted_iota(jnp.int32, (SUBLANES, x.shape[1]), 0)
    head = jnp.where(rows < s, fill, rolled[:SUBLANES])
    return jnp.concatenate([head, rolled[SUBLANES:]], axis=0)


def _rglru_kernel(xr_ref, gr_ref, wc_ref, bc_ref, wax_ref, ba_ref, bx_ref, lam_ref, y_ref,
                  tail_ref, hc_ref, *, n_conv, bd):
    t = pl.program_id(2)

    @pl.when(t == 0)
    def _():
        tail_ref[...] = jnp.zeros_like(tail_ref)
        hc_ref[...] = jnp.zeros_like(hc_ref)

    x = xr_ref[...]
    tc, cw = x.shape
    tail = tail_ref[...]
    wc = wc_ref[...]
    xc = wc[n_conv - 1:n_conv] * x + bc_ref[...]
    for k in range(1, n_conv):
        prev = pltpu.roll(tail, k, axis=0)
        xc = xc + wc[n_conv - 1 - k:n_conv - k] * _shift_rows(x, k, prev)
    tail_ref[...] = x[tc - SUBLANES:]

    xb = xc.astype(BF16)
    r_parts, i_parts = [], []
    for gblk in range(cw // bd):
        ga = _dot(xb[:, gblk * bd:(gblk + 1) * bd], wax_ref[gblk])
        r_parts.append(ga[:, :bd])
        i_parts.append(ga[:, bd:])
    r = jax.nn.sigmoid(jnp.concatenate(r_parts, axis=1) + ba_ref[...])
    ig = jax.nn.sigmoid(jnp.concatenate(i_parts, axis=1) + bx_ref[...])

    nl = -lam_ref[...]
    softplus_nl = jnp.maximum(nl, 0.0) + jnp.log(1.0 + jnp.exp(-jnp.abs(nl)))
    log_a = (-RG_C) * r * softplus_nl
    a = jnp.exp(log_a)
    mult = jnp.sqrt(jnp.tanh(-log_a) * (a * a + 1.0))
    u = xc * ig * mult

    ca, cb = a, u
    s = 1
    while s < tc:
        cb = ca * _shift_rows(cb, s, 0.0) + cb
        ca = ca * _shift_rows(ca, s, 1.0)
        s *= 2
    h = ca * hc_ref[...] + cb
    hc_ref[...] = h[tc - 1:tc]
    y_ref[...] = (h * gr_ref[...].astype(F32)).astype(BF16)


def _rglru(xr, gr, w_conv, b_conv, wax, b_a, b_x, lam, *, tc, cw):
    B, S, C = xr.shape
    n_conv = w_conv.shape[0]
    bd = wax.shape[1]
    row = lambda b, c, t: (0, c)
    act = pl.BlockSpec((None, tc, cw), lambda b, c, t: (b, t, c))
    kern = functools.partial(_rglru_kernel, n_conv=n_conv, bd=bd)
    return pl.pallas_call(
        kern,
        grid=(B, C // cw, S // tc),
        in_specs=[
            act, act,
            pl.BlockSpec((n_conv, cw), row),
            pl.BlockSpec((1, cw), row),
            pl.BlockSpec((cw // bd, bd, 2 * bd), lambda b, c, t: (c, 0, 0)),
            pl.BlockSpec((1, cw), row),
            pl.BlockSpec((1, cw), row),
            pl.BlockSpec((1, cw), row),
        ],
        out_specs=act,
        out_shape=jax.ShapeDtypeStruct((B, S, C), BF16),
        scratch_shapes=[pltpu.VMEM((SUBLANES, cw), F32), pltpu.VMEM((1, cw), F32)],
        compiler_params=_params(("arbitrary", "arbitrary", "arbitrary")),
        name="rglru",
    )(xr, gr, w_conv, b_conv, wax, b_a, b_x, lam)


def _sb_scores(z, mask):
    neg_abs = lax.bitcast_convert_type(lax.bitcast_convert_type(z, jnp.uint32) | jnp.uint32(0x80000000), F32)
    softplus = jnp.maximum(z, 0.0) + jnp.log(1.0 + jnp.exp2(neg_abs)) * LOG2E
    log_beta = z - softplus
    if mask is not None:
        softplus = jnp.where(mask, softplus, 0.0)
    hi = softplus.astype(BF16)
    lo = (softplus - hi.astype(F32)).astype(BF16)
    return log_beta, jnp.concatenate([hi, lo], axis=1), jnp.sum(softplus, axis=1, keepdims=True)


def _sb_weights(log_beta, suffix, later, mask):
    att = jnp.exp2(log_beta - suffix - later)
    if mask is not None:
        att = jnp.where(mask, att, 0.0)
    return att.astype(BF16)


def _sb_attn_kernel(q_ref, k_ref, v_ref, o_ref, *, dh):
    i = pl.program_id(2)
    tq = q_ref.shape[0]
    tk = tq
    n_heads = q_ref.shape[1] // dh
    rows = lax.broadcasted_iota(jnp.int32, (tk, tk), 0)
    cols = lax.broadcasted_iota(jnp.int32, (tk, tk), 1)
    strict = (rows > cols).astype(BF16)
    strict2 = jnp.concatenate([strict, strict], axis=0)
    causal = cols < rows

    def sweep(s0, carry, mask):
        heads = range(n_heads)
        lanes = [slice(h * dh, (h + 1) * dh) for h in heads]
        zs = [lax.dot_general(q_ref[:, lanes[h]], k_ref[pl.ds(s0, tk), lanes[h]],
                              (((1,), (1,)), ((), ())), preferred_element_type=F32) for h in heads]
        scores = [_sb_scores(zs[h], mask) for h in heads]
        suffixes = [_dot(scores[h][1], strict2) for h in heads]
        atts = [_sb_weights(scores[h][0], suffixes[h], carry[h][1], mask) for h in heads]
        pvs = [_dot(atts[h], v_ref[pl.ds(s0, tk), lanes[h]]) for h in heads]
        return tuple((carry[h][0] + pvs[h], carry[h][1] + scores[h][2]) for h in heads)

    carry = tuple((jnp.zeros((tq, dh), F32), jnp.zeros((tq, 1), F32)) for _ in range(n_heads))
    carry = sweep(pl.multiple_of(i * tq, tq), carry, causal)
    carry = lax.fori_loop(0, i, lambda n, c: sweep(pl.multiple_of((i - 1 - n) * tk, tk), c, None), carry)
    for h in range(n_heads):
        o_ref[:, h * dh:(h + 1) * dh] = carry[h][0].astype(BF16)


def _sb_attn(qkv, *, n_heads, head_dim, tq, heads_per_step):
    B, S, _ = qkv.shape
    hw = heads_per_step * head_dim
    n_groups = n_heads // heads_per_step
    kern = functools.partial(_sb_attn_kernel, dh=head_dim)
    return pl.pallas_call(
        kern,
        grid=(B, n_groups, S // tq),
        in_specs=[
            pl.BlockSpec((None, tq, hw), lambda b, h, i: (b, i, h)),
            pl.BlockSpec((None, S, hw), lambda b, h, i: (b, 0, n_groups + h)),
            pl.BlockSpec((None, S, hw), lambda b, h, i: (b, 0, 2 * n_groups + h)),
        ],
        out_specs=pl.BlockSpec((None, tq, hw), lambda b, h, i: (b, i, h)),
        out_shape=jax.ShapeDtypeStruct((B, S, n_heads * head_dim), BF16),
        compiler_params=_params(("arbitrary", "arbitrary", "arbitrary")),
        name="sb_attn",
    )(qkv, qkv, qkv)


def _col_tiles(n):
    step = MXU_COLS if n % MXU_COLS == 0 else n
    return [slice(c, c + step) for c in range(0, n, step)]


def _merge_kernel(yr_ref, ya_ref, w0_ref, w1_ref, g0_ref, g1_ref, o_ref):
    for cols in _col_tiles(o_ref.shape[1]):
        d0 = _dot(yr_ref[...], w0_ref[:, cols])
        d1 = _dot(ya_ref[...], w1_ref[:, cols])
        o_ref[:, cols] = (g0_ref[:, cols].astype(F32) * d0 + g1_ref[:, cols].astype(F32) * d1).astype(BF16)


def _merge(y_rec, y_att, w_branch, gates, *, tm, tn):
    T, C = y_rec.shape
    D = w_branch.shape[1]
    nj = D // tn
    return pl.pallas_call(
        _merge_kernel,
        grid=(T // tm, nj),
        in_specs=[
            pl.BlockSpec((tm, C), lambda i, j: (i, 0)),
            pl.BlockSpec((tm, C), lambda i, j: (i, 0)),
            pl.BlockSpec((C, tn), lambda i, j: (0, j)),
            pl.BlockSpec((C, tn), lambda i, j: (1, j)),
            pl.BlockSpec((tm, tn), lambda i, j: (i, j)),
            pl.BlockSpec((tm, tn), lambda i, j: (i, j + nj)),
        ],
        out_specs=pl.BlockSpec((tm, tn), lambda i, j: (i, j)),
        out_shape=jax.ShapeDtypeStruct((T, D), BF16),
        compiler_params=_params(("arbitrary", "arbitrary")),
        name="merge",
    )(y_rec, y_att, w_branch, w_branch, gates, gates)


def _emit_residual(x_new, cols, g_ref, o_ref, xg_ref):
    o_ref[:, cols] = x_new
    xg_ref[:, cols] = (x_new * g_ref[:, cols]).astype(BF16)
    return _sum_sq(x_new)


def _accumulate_r(ssq, ssq_ref, r_ref, j, nj, width):
    @pl.when(j == 0)
    def _():
        ssq_ref[...] = ssq

    @pl.when(j > 0)
    def _():
        ssq_ref[...] += ssq

    @pl.when(j == nj - 1)
    def _():
        r_ref[...] = lax.rsqrt(ssq_ref[...] * (1.0 / width) + EPS)


def _matmul_res_kernel(a_ref, w_ref, x_ref, g_ref, o_ref, xg_ref, r_ref, acc_ref, ssq_ref):
    j, k = pl.program_id(1), pl.program_id(2)
    nj, nk = pl.num_programs(1), pl.num_programs(2)
    tn = o_ref.shape[1]

    @pl.when(k == 0)
    def _():
        for cols in _col_tiles(tn):
            acc_ref[:, cols] = x_ref[:, cols] + _dot(a_ref[...], w_ref[:, cols])

    @pl.when((k > 0) & (k < nk - 1))
    def _():
        for cols in _col_tiles(tn):
            acc_ref[:, cols] += _dot(a_ref[...], w_ref[:, cols])

    @pl.when(k == nk - 1)
    def _():
        ssq = 0.0
        for cols in _col_tiles(tn):
            ssq = ssq + _emit_residual(acc_ref[:, cols] + _dot(a_ref[...], w_ref[:, cols]), cols,
                                       g_ref, o_ref, xg_ref)
        _accumulate_r(ssq, ssq_ref, r_ref, j, nj, nj * tn)


def _matmul_res_single_kernel(a_ref, w_ref, x_ref, g_ref, o_ref, xg_ref, r_ref, ssq_ref):
    j, nj = pl.program_id(1), pl.num_programs(1)
    tn = o_ref.shape[1]
    ssq = 0.0
    for cols in _col_tiles(tn):
        ssq = ssq + _emit_residual(x_ref[:, cols] + _dot(a_ref[...], w_ref[:, cols]), cols, g_ref, o_ref, xg_ref)
    _accumulate_r(ssq, ssq_ref, r_ref, j, nj, nj * tn)


def _matmul_res(a, w, x, g_next, *, tm, tn, tk):
    T, K = a.shape
    N = w.shape[1]
    out_shape = [jax.ShapeDtypeStruct((T, N), F32), jax.ShapeDtypeStruct((T, N), BF16),
                 jax.ShapeDtypeStruct((T, 1), F32)]
    if tk == K:
        tile = pl.BlockSpec((tm, tn), lambda i, j: (i, j))
        return pl.pallas_call(
            _matmul_res_single_kernel,
            grid=(T // tm, N // tn),
            in_specs=[
                pl.BlockSpec((tm, K), lambda i, j: (i, 0)),
                pl.BlockSpec((K, tn), lambda i, j: (0, j)),
                tile,
                pl.BlockSpec((1, tn), lambda i, j: (0, j)),
            ],
            out_specs=[tile, tile, pl.BlockSpec((tm, 1), lambda i, j: (i, 0))],
            out_shape=out_shape,
            scratch_shapes=[pltpu.VMEM((tm, 1), F32)],
            compiler_params=_params(("arbitrary", "arbitrary")),
            name="matmul_res",
        )(a, w, x, g_next)
    assert K // tk >= 2
    tile = pl.BlockSpec((tm, tn), lambda i, j, k: (i, j))
    return pl.pallas_call(
        _matmul_res_kernel,
        grid=(T // tm, N // tn, K // tk),
        in_specs=[
            pl.BlockSpec((tm, tk), lambda i, j, k: (i, k)),
            pl.BlockSpec((tk, tn), lambda i, j, k: (k, j)),
            tile,
            pl.BlockSpec((1, tn), lambda i, j, k: (0, j)),
        ],
        out_specs=[tile, tile, pl.BlockSpec((tm, 1), lambda i, j, k: (i, 0))],
        out_shape=out_shape,
        scratch_shapes=[pltpu.VMEM((tm, tn), F32), pltpu.VMEM((tm, 1), F32)],
        compiler_params=_params(("arbitrary", "arbitrary", "arbitrary")),
        name="matmul_res_k",
    )(a, w, x, g_next)


def _ffn_up_kernel(xg_ref, r_ref, wa_ref, wv_ref, ca_ref, cv_ref, ba_ref, bv_ref, o_ref, carry_ref, ubuf_ref,
                   *, n_conv, blocks_per_seq):
    i = pl.program_id(0)
    j = pl.program_id(1)
    tm, tn = o_ref.shape

    @pl.when(i == 0)
    def _():
        carry_ref[j] = jnp.zeros(carry_ref.shape[1:], F32)

    keep = (i % blocks_per_seq != 0).astype(F32)
    r = r_ref[...]

    def conv(half, w_ref, c_ref, b_ref, cols):
        up = _dot(xg_ref[...], w_ref[:, cols]) * r
        width = up.shape[1]
        ubuf_ref[half, 0:SUBLANES, 0:width] = carry_ref[j, half, :, cols] * keep
        ubuf_ref[half, SUBLANES:, 0:width] = up
        carry_ref[j, half, :, cols] = up[tm - SUBLANES:]
        c = c_ref[:, cols]
        out = c[n_conv - 1:n_conv] * up + b_ref[:, cols]
        for k in range(1, n_conv):
            out = out + c[n_conv - 1 - k:n_conv - k] * ubuf_ref[half, SUBLANES - k:SUBLANES - k + tm, 0:width]
        return out

    for cols in _col_tiles(tn):
        v_half = conv(1, wv_ref, cv_ref, bv_ref, cols)
        a_half = conv(0, wa_ref, ca_ref, ba_ref, cols)
        o_ref[:, cols] = (jax.nn.gelu(a_half) * v_half).astype(BF16)


def _ffn_up(xg, r, w_up, w_conv, b_conv, *, seq, tm, tn):
    T, D = xg.shape
    F = w_up.shape[1] // 2
    nj = F // tn
    n_conv = w_conv.shape[0]
    assert seq % tm == 0 and n_conv <= SUBLANES
    kern = functools.partial(_ffn_up_kernel, n_conv=n_conv, blocks_per_seq=seq // tm)
    lo = lambda i, j: (0, j)
    hi = lambda i, j: (0, j + nj)
    return pl.pallas_call(
        kern,
        grid=(T // tm, nj),
        in_specs=[
            pl.BlockSpec((tm, D), lambda i, j: (i, 0)),
            pl.BlockSpec((tm, 1), lambda i, j: (i, 0)),
            pl.BlockSpec((D, tn), lo),
            pl.BlockSpec((D, tn), hi),
            pl.BlockSpec((n_conv, tn), lo),
            pl.BlockSpec((n_conv, tn), hi),
            pl.BlockSpec((1, tn), lo),
            pl.BlockSpec((1, tn), hi),
        ],
        out_specs=pl.BlockSpec((tm, tn), lambda i, j: (i, j)),
        out_shape=jax.ShapeDtypeStruct((T, F), BF16),
        scratch_shapes=[pltpu.VMEM((nj, 2, SUBLANES, tn), F32),
                        pltpu.VMEM((2, tm + SUBLANES, min(tn, MXU_COLS)), F32)],
        compiler_params=_params(("arbitrary", "arbitrary")),
        name="ffn_up",
    )(xg, r, w_up, w_up, w_conv, w_conv, b_conv, b_conv)


def _ple_kernel(x_ref, xg_ref, r_ref, p_ref, wp_ref, wg_ref, b_ref, gn_ref, o_ref, xgn_ref, rn_ref, ssq_ref):
    j, nj = pl.program_id(1), pl.num_programs(1)
    tn = o_ref.shape[1]
    pb = p_ref[...].astype(BF16)
    r = r_ref[...]
    ssq = 0.0
    for cols in _col_tiles(tn):
        e = _dot(pb, wp_ref[:, cols])
        gate = jax.nn.sigmoid(_dot(xg_ref[...], wg_ref[:, cols]) * r + b_ref[:, cols])
        ssq = ssq + _emit_residual(x_ref[:, cols] + gate * e, cols, gn_ref, o_ref, xgn_ref)
    _accumulate_r(ssq, ssq_ref, rn_ref, j, nj, nj * tn)


def _ple(x, xg, r, p, w_ple, w_gate, b_gate, g_next, *, tm, tn):
    T, D = x.shape
    P = p.shape[1]
    tile = pl.BlockSpec((tm, tn), lambda i, j: (i, j))
    vec = pl.BlockSpec((1, tn), lambda i, j: (0, j))
    col = pl.BlockSpec((tm, 1), lambda i, j: (i, 0))
    return pl.pallas_call(
        _ple_kernel,
        grid=(T // tm, D // tn),
        in_specs=[
            tile,
            pl.BlockSpec((tm, D), lambda i, j: (i, 0)),
            col,
            pl.BlockSpec((tm, P), lambda i, j: (i, 0)),
            pl.BlockSpec((P, tn), lambda i, j: (0, j)),
            pl.BlockSpec((D, tn), lambda i, j: (0, j)),
            vec,
            vec,
        ],
        out_specs=[tile, tile, col],
        out_shape=[jax.ShapeDtypeStruct((T, D), F32), jax.ShapeDtypeStruct((T, D), BF16),
                   jax.ShapeDtypeStruct((T, 1), F32)],
        scratch_shapes=[pltpu.VMEM((tm, 1), F32)],
        compiler_params=_params(("arbitrary", "arbitrary")),
        name="ple",
    )(x, xg, r, p, w_ple, w_gate, b_gate, g_next)


def _norm_kernel(x_ref, r_ref, g_ref, o_ref):
    o_ref[...] = x_ref[...] * r_ref[...] * g_ref[...]


def _final_norm(x, r, g, *, tm):
    T, D = x.shape
    return pl.pallas_call(
        _norm_kernel,
        grid=(T // tm,),
        in_specs=[pl.BlockSpec((tm, D), lambda i: (i, 0)), pl.BlockSpec((tm, 1), lambda i: (i, 0)),
                  pl.BlockSpec((1, D), lambda i: (0, 0))],
        out_specs=pl.BlockSpec((tm, D), lambda i: (i, 0)),
        out_shape=jax.ShapeDtypeStruct((T, D), F32),
        compiler_params=_params(("arbitrary",)),
        name="final_norm",
    )(x, r, g)


def _tile(n, want):
    t = min(n, want)
    while n % t:
        t //= 2
    return t


def kernel(x, p, g_mix, w_in, w_rg_conv, b_rg_conv, w_rg_a, b_rg_a, w_rg_x, b_rg_x, rg_lambda,
           w_branch, w_branch_gate, b_branch_gate, w_out, g_ffn, w_up, w_ffn_conv, b_ffn_conv,
           w_down, g_ple, w_ple, w_ple_gate, b_ple_gate, g_final):
    B, S, D = x.shape
    T = B * S
    depth = w_in.shape[0]
    rg_width = w_rg_conv.shape[2]
    sb_width = (w_in.shape[2] - 2 * rg_width) // 3
    head_dim = SB_HEAD_DIM
    n_heads = sb_width // head_dim
    d_ff = w_down.shape[1]

    tm = _tile(S, 1024)
    row = lambda a: a.reshape(1, -1)
    w_branch_rows = w_branch.reshape(depth, -1, D)

    xf = x.reshape(T, D)
    xg, r = _prep(xf, row(g_mix[0]), tm=_tile(T, 256))
    for l in range(depth):
        xr, gr, qkv, gates = _mix_in(
            xg, r, _cast_layer(w_in, l), _cast_layer(w_branch_gate, l), row(b_branch_gate[l]),
            rg_width=rg_width, sb_width=sb_width, head_dim=head_dim, tm=tm, tn=_tile(rg_width, 512))

        wax = jnp.concatenate([w_rg_a[l], w_rg_x[l]], axis=2).astype(BF16)
        y_rec = _rglru(
            xr.reshape(B, S, rg_width), gr.reshape(B, S, rg_width), w_rg_conv[l], row(b_rg_conv[l]),
            wax, row(b_rg_a[l]), row(b_rg_x[l]), row(rg_lambda[l]),
            tc=_tile(S, 256), cw=_tile(rg_width, 512))
        y_att = _sb_attn(qkv.reshape(B, S, 3 * sb_width), n_heads=n_heads, head_dim=head_dim,
                         tq=_tile(S, 256), heads_per_step=min(8, n_heads))

        merged = _merge(y_rec.reshape(T, rg_width), y_att.reshape(T, sb_width),
                        _cast_layer(w_branch_rows, l), gates, tm=tm, tn=_tile(D, 512))
        xf, xg, r = _matmul_res(merged, _cast_layer(w_out, l), xf, row(g_ffn[l]),
                                tm=tm, tn=_tile(D, 512), tk=D)

        act = _ffn_up(xg, r, _cast_layer(w_up, l), w_ffn_conv[l], row(b_ffn_conv[l]),
                      seq=S, tm=tm, tn=_tile(d_ff, 3 * MXU_COLS))
        xf, xg, r = _matmul_res(act, _cast_layer(w_down, l), xf, row(g_ple[l]),
                                tm=tm, tn=_tile(D, 1024), tk=_tile(d_ff, 2048))

        g_next = g_mix[l + 1] if l + 1 < depth else g_final
        xf, xg, r = _ple(xf, xg, r, p[l].reshape(T, -1), _cast_layer(w_ple, l), _cast_layer(w_ple_gate, l),
                         row(b_ple_gate[l]), row(g_next), tm=tm, tn=_tile(D, 512))

    return _final_norm(xf, r, row(g_final), tm=_tile(T, 256)).reshape(B, S, D)
```
